```python
import math
import jax, jax.numpy as jnp
from jax import lax
import numpy as np

D_MODEL = 2048
BATCH = 4
SEQ = 2048
DEPTH = 2
DEC_BATCH = 128
DEC_SEQ = 1
PAST_LEN = 2048
PAGE_SIZE = 128

HEAD_DIM = 128
N_HEADS = D_MODEL // HEAD_DIM
HEADS_B = (5 * N_HEADS) // 16
HEADS_C = (5 * N_HEADS) // 16
HEADS_A = N_HEADS - HEADS_B - HEADS_C
DIFF_HALF = HEAD_DIM // 2
ROT_A = HEAD_DIM // 4
ROT_B = DIFF_HALF // 4
ROPE_THETA = 500000.0
MOBA_BLOCK = 256
MOBA_TOPK = 3
MOBA_Q_CHUNK = 16
DENSE_Q_CHUNK = 128
RMS_EPS = 1e-6
NEG_INF = -1e30

kernel_name = 'hymba_moba_diff_stickbreak_decode_step'


def rms_norm(x, g):
    xf = x.astype(jnp.float32)
    xf = xf * lax.rsqrt(jnp.mean(xf * xf, axis=-1, keepdims=True) + RMS_EPS)
    return (xf * g.astype(jnp.float32)).astype(x.dtype)


def rope(x, pos, rot):
    half = rot // 2
    inv = ROPE_THETA ** (-jnp.arange(half, dtype=jnp.float32) * (2.0 / rot))
    ang = pos.astype(jnp.float32)[:, None] * inv[None, :]
    cos = jnp.cos(ang)[None, :, None, :]
    sin = jnp.sin(ang)[None, :, None, :]
    xf = x.astype(jnp.float32)
    x1, x2 = xf[..., :half], xf[..., half:rot]
    out = jnp.concatenate([x1 * cos - x2 * sin, x2 * cos + x1 * sin, xf[..., rot:]], axis=-1)
    return out.astype(x.dtype)


def sweep_queries(fn, q, q_pos, chunk):
    bsz, t_len, n_h, d = q.shape
    n = t_len // chunk
    qc = jnp.moveaxis(q.reshape(bsz, n, chunk, n_h, d), 1, 0)
    pc = q_pos.reshape(n, chunk)
    out = lax.map(lambda a: fn(a[0], a[1]), (qc, pc))
    return jnp.moveaxis(out, 0, 1).reshape(bsz, t_len, n_h, out.shape[-1])


def moba_attention(q, k, v, q_pos):
    bsz, l_len, n_h, d = k.shape
    n_blk = -(-l_len // MOBA_BLOCK)
    pad = ((0, 0), (0, n_blk * MOBA_BLOCK - l_len), (0, 0), (0, 0))
    kb = jnp.pad(k, pad).reshape(bsz, n_blk, MOBA_BLOCK, n_h, d)
    vb = jnp.pad(v, pad).reshape(bsz, n_blk, MOBA_BLOCK, n_h, d)
    k_mean = jnp.mean(kb.astype(jnp.float32), axis=2)
    kb = kb.transpose(0, 3, 1, 2, 4)
    vb = vb.transpose(0, 3, 1, 2, 4)
    n_sel = min(MOBA_TOPK, n_blk)
    blk = jnp.arange(n_blk, dtype=jnp.int32)
    offs = jnp.arange(MOBA_BLOCK, dtype=jnp.int32)
    gather = jax.vmap(jax.vmap(lambda t, i: t[i]))
    scale = d ** -0.5

    def block_fn(qc, pc):
        own = pc // MOBA_BLOCK
        gate = jnp.einsum('bchd,bnhd->bhcn', qc.astype(jnp.float32), k_mean)
        gate = jnp.where(blk[None, :] < own[:, None], gate, NEG_INF)
        _, sel = lax.top_k(gate, n_sel)
        own_b = jnp.broadcast_to(own[:, None], sel.shape[:-1] + (1,))
        ids = jnp.concatenate([sel, own_b], axis=-1)
        slot_ok = jnp.concatenate([sel < own[:, None], jnp.ones(own_b.shape, dtype=bool)], axis=-1)
        kg = gather(kb, ids)
        vg = gather(vb, ids)
        kpos = ids[..., None] * MOBA_BLOCK + offs
        valid = slot_ok[..., None] & (kpos <= pc[:, None, None])
        s = jnp.einsum('bchd,bhcmkd->bhcmk', qc, kg, preferred_element_type=jnp.float32) * scale
        s = jnp.where(valid, s, NEG_INF)
        p = jax.nn.softmax(s.reshape(s.shape[:3] + (-1,)), axis=-1).reshape(s.shape)
        return jnp.einsum('bhcmk,bhcmkd->bchd', p.astype(vg.dtype), vg)

    return sweep_queries(block_fn, q, q_pos, math.gcd(q.shape[1], MOBA_Q_CHUNK))


def diff_attention(q, k, v, q_pos, lam):
    kpos = jnp.arange(k.shape[1], dtype=jnp.int32)
    k1, k2 = k[..., :DIFF_HALF], k[..., DIFF_HALF:]
    scale = DIFF_HALF ** -0.5

    def block_fn(qc, pc):
        mask = kpos[None, :] <= pc[:, None]

        def probs(qh, kh):
            s = jnp.einsum('bchd,blhd->bhcl', qh, kh, preferred_element_type=jnp.float32) * scale
            return jax.nn.softmax(jnp.where(mask, s, NEG_INF), axis=-1)

        w = probs(qc[..., :DIFF_HALF], k1) - lam * probs(qc[..., DIFF_HALF:], k2)
        return jnp.einsum('bhcl,blhd->bchd', w.astype(v.dtype), v)

    return sweep_queries(block_fn, q, q_pos, math.gcd(q.shape[1], DENSE_Q_CHUNK))


def stick_breaking_attention(q, k, v, q_pos):
    kpos = jnp.arange(k.shape[1], dtype=jnp.int32)
    scale = q.shape[-1] ** -0.5

    def block_fn(qc, pc):
        earlier = kpos[None, :] < pc[:, None]
        z = jnp.einsum('bchd,blhd->bhcl', qc, k, preferred_element_type=jnp.float32) * scale
        log_keep = jnp.where(earlier, jax.nn.log_sigmoid(-z), 0.0)
        between = lax.cumsum(log_keep, axis=3, reverse=True) - log_keep
        a = jnp.where(earlier, jnp.exp(jax.nn.log_sigmoid(z) + between), 0.0)
        return jnp.einsum('bhcl,blhd->bchd', a.astype(v.dtype), v)

    return sweep_queries(block_fn, q, q_pos, math.gcd(q.shape[1], DENSE_Q_CHUNK))


def gather_pages(cache, layer, page_table, h0, h1):
    rows = cache[layer, page_table, :, h0:h1]
    return rows.reshape(rows.shape[0], -1, h1 - h0, HEAD_DIM)


def decoder_layer(x, pos, paged, layer, g_pre, w_in, w_out, g_post, lam, lam_init, diff_gain):
    bsz, t_len, _ = x.shape
    h = rms_norm(x, g_pre)
    proj = jnp.einsum('btd,de->bte', h, w_in).reshape(bsz, t_len, 4, N_HEADS, HEAD_DIM)
    q, k, v, gate = proj[:, :, 0], proj[:, :, 1], proj[:, :, 2], proj[:, :, 3]
    a_end, b_end = HEADS_A, HEADS_A + HEADS_B

    def rotate(t):
        t_b = t[:, :, a_end:b_end].reshape(bsz, t_len, 2 * HEADS_B, DIFF_HALF)
        t_b = rope(t_b, pos, ROT_B).reshape(bsz, t_len, HEADS_B, HEAD_DIM)
        return jnp.concatenate([rope(t[:, :, :a_end], pos, ROT_A), t_b, t[:, :, b_end:]], axis=2)

    q, k = rotate(q), rotate(k)

    def group_kv(h0, h1):
        k_new, v_new = k[:, :, h0:h1], v[:, :, h0:h1]
        if paged is None:
            return k_new, v_new
        cache_k, cache_v, page_table = paged
        k_all = jnp.concatenate([gather_pages(cache_k, layer, page_table, h0, h1), k_new], axis=1)
        v_all = jnp.concatenate([gather_pages(cache_v, layer, page_table, h0, h1), v_new], axis=1)
        return k_all, v_all

    k_a, v_a = group_kv(0, a_end)
    o_a = moba_attention(q[:, :, :a_end], k_a, v_a, pos)
    k_b, v_b = group_kv(a_end, b_end)
    o_b = diff_attention(q[:, :, a_end:b_end], k_b, v_b, pos, lam)
    o_b = rms_norm(o_b, diff_gain) * (1.0 - lam_init)
    k_c, v_c = group_kv(b_end, N_HEADS)
    o_c = stick_breaking_attention(q[:, :, b_end:], k_c, v_c, pos)
    o = jnp.concatenate([o_a, o_b, o_c], axis=2) * jax.nn.silu(gate)
    y = jnp.einsum('btd,de->bte', o.reshape(bsz, t_len, D_MODEL), w_out)
    return x + rms_norm(y, g_post), k, v


def setup_inputs(seed: int = 0) -> dict:
    key = jax.random.key(seed)
    ks = jax.random.split(key, 15)
    n_pages = PAST_LEN // PAGE_SIZE
    n_used = DEC_BATCH * n_pages
    n_pool = n_used + max(1, n_used // 4)
    cache_shape = (DEPTH, n_pool, PAGE_SIZE, N_HEADS, HEAD_DIM)
    page_table = jax.random.permutation(ks[4], n_pool)[:n_used].reshape(DEC_BATCH, n_pages).astype(jnp.int32)
    return {
        'x_prompt': jax.random.normal(ks[0], (BATCH, SEQ, D_MODEL), jnp.float32),
        'x_sample': jax.random.normal(ks[1], (DEC_BATCH, DEC_SEQ, D_MODEL), jnp.float32),
        'cache_k': jax.random.normal(ks[2], cache_shape, jnp.float32),
        'cache_v': jax.random.normal(ks[3], cache_shape, jnp.float32),
        'page_table': page_table,
        'g_pre': 1.0 + 0.05 * jax.random.normal(ks[5], (DEPTH, D_MODEL), jnp.float32),
        'w_in': jax.random.normal(ks[6], (DEPTH, D_MODEL, 4 * D_MODEL), jnp.float32) * D_MODEL ** -0.5,
        'w_out': jax.random.normal(ks[7], (DEPTH, D_MODEL, D_MODEL), jnp.float32) * D_MODEL ** -0.5,
        'g_post': 1.0 + 0.05 * jax.random.normal(ks[8], (DEPTH, D_MODEL), jnp.float32),
        'lam_q1': 0.1 * jax.random.normal(ks[9], (DEPTH, DIFF_HALF), jnp.float32),
        'lam_k1': 0.1 * jax.random.normal(ks[10], (DEPTH, DIFF_HALF), jnp.float32),
        'lam_q2': 0.1 * jax.random.normal(ks[11], (DEPTH, DIFF_HALF), jnp.float32),
        'lam_k2': 0.1 * jax.random.normal(ks[12], (DEPTH, DIFF_HALF), jnp.float32),
        'diff_gain': 1.0 + 0.05 * jax.random.normal(ks[13], (DEPTH, HEAD_DIM), jnp.float32),
    }


def reference(x_prompt, x_sample, cache_k, cache_v, page_table, g_pre, w_in, w_out, g_post,
              lam_q1, lam_k1, lam_q2, lam_k2, diff_gain):
    past_len = page_table.shape[1] * cache_k.shape[2]
    pos_p = jnp.arange(x_prompt.shape[1], dtype=jnp.int32)
    pos_s = past_len + jnp.arange(x_sample.shape[1], dtype=jnp.int32)
    paged = (cache_k, cache_v, page_table)
    y_p, y_s = x_prompt, x_sample
    k_p, v_p, k_s, v_s = [], [], [], []
    for layer in range(DEPTH):
        lam_init = 0.8 - 0.6 * math.exp(-0.3 * layer)
        lam = (jnp.exp(jnp.sum(lam_q1[layer].astype(jnp.float32) * lam_k1[layer].astype(jnp.float32)))
               - jnp.exp(jnp.sum(lam_q2[layer].astype(jnp.float32) * lam_k2[layer].astype(jnp.float32)))
               + lam_init)
        params = (g_pre[layer], w_in[layer], w_out[layer], g_post[layer], lam, lam_init, diff_gain[layer])
        y_p, kn, vn = decoder_layer(y_p, pos_p, None, layer, *params)
        k_p.append(kn)
        v_p.append(vn)
        y_s, kn, vn = decoder_layer(y_s, pos_s, paged, layer, *params)
        k_s.append(kn)
        v_s.append(vn)
    return (y_p, y_s, jnp.stack(k_p), jnp.stack(v_p), jnp.stack(k_s), jnp.stack(v_s))
```

```python
import functools
import math

import jax
import jax.numpy as jnp
from jax import lax
from jax.experimental import pallas as pl
from jax.experimental.pallas import tpu as pltpu

D_MODEL = 2048
HEAD_DIM = 128
N_HEADS = D_MODEL // HEAD_DIM
HEADS_B = (5 * N_HEADS) // 16
HEADS_C = (5 * N_HEADS) // 16
HEADS_A = N_HEADS - HEADS_B - HEADS_C
B_END = HEADS_A + HEADS_B
DIFF_HALF = HEAD_DIM // 2
ROT_A = HEAD_DIM // 4
ROT_B = DIFF_HALF // 4
ROPE_THETA = 500000.0
MOBA_BLOCK = 256
MOBA_TOPK = 3
RMS_EPS = 1e-6
NEG_INF = -1e30

F32 = jnp.float32
BF16 = jnp.bfloat16

VMEM_LIMIT_BYTES = 56 * 1024 * 1024

IN_TN = 512
IN_TM_MAX = 1024
OUT_TM_MAX = 512
ATT_T = MOBA_BLOCK
DEC_ROWS = 32


def _nt_dot(a, b):
    return lax.dot_general(a, b, (((1,), (1,)), ((), ())), preferred_element_type=F32)


def _split2(x):
    hi = x.astype(BF16)
    lo = (x - hi.astype(F32)).astype(BF16)
    return hi, lo


def _softplus(z):
    return jnp.maximum(z, 0.0) + jnp.log1p(jnp.exp(-jnp.abs(z)))


def _rope_apply(x, c_ref, s1_ref, s2_ref, half):
    return (x * c_ref[...]
            + pltpu.roll(x, HEAD_DIM - half, axis=1) * s1_ref[...]
            + pltpu.roll(x, half, axis=1) * s2_ref[...])


def _in_proj_kernel(x_ref, g_ref, w_ref, ca_ref, s1a_ref, s2a_ref, cb_ref, s1b_ref, s2b_ref,
                    o_ref, h_scr):
    j = pl.program_id(1)

    @pl.when(j == 0)
    def _():
        xf = x_ref[...]
        ms = jnp.mean(xf * xf, axis=-1, keepdims=True)
        h_scr[...] = (xf * lax.rsqrt(ms + RMS_EPS) * g_ref[...]).astype(BF16)

    acc = jnp.dot(h_scr[...], w_ref[...], preferred_element_type=F32)
    tiles_per_section = D_MODEL // IN_TN
    heads_per_tile = IN_TN // HEAD_DIM
    section = j // tiles_per_section
    head0 = (j % tiles_per_section) * heads_per_tile
    rotated = section < 2
    for c in range(heads_per_tile):
        head = head0 + c
        sl = slice(c * HEAD_DIM, (c + 1) * HEAD_DIM)
        xa = acc[:, sl]
        is_a = jnp.logical_and(rotated, head < HEADS_A)
        is_b = jnp.logical_and(rotated, jnp.logical_and(head >= HEADS_A, head < B_END))

        @pl.when(is_a)
        def _():
            o_ref[:, sl] = _rope_apply(xa, ca_ref, s1a_ref, s2a_ref, ROT_A // 2)

        @pl.when(is_b)
        def _():
            o_ref[:, sl] = _rope_apply(xa, cb_ref, s1b_ref, s2b_ref, ROT_B // 2)

        @pl.when(jnp.logical_not(jnp.logical_or(is_a, is_b)))
        def _():
            o_ref[:, sl] = xa


def _in_proj(x2d, g_pre, w_bf16, tables):
    m = x2d.shape[0]
    tm = min(m, IN_TM_MAX)
    n_out = w_bf16.shape[1]
    tab_rows = tables[0].shape[0]
    tab_blocks = tab_rows // tm
    tab_spec = pl.BlockSpec((tm, HEAD_DIM), lambda i, j: (i % tab_blocks, 0))
    return pl.pallas_call(
        _in_proj_kernel,
        grid=(m // tm, n_out // IN_TN),
        in_specs=[
            pl.BlockSpec((tm, D_MODEL), lambda i, j: (i, 0)),
            pl.BlockSpec((1, D_MODEL), lambda i, j: (0, 0)),
            pl.BlockSpec((D_MODEL, IN_TN), lambda i, j: (0, j)),
        ] + [tab_spec] * 6,
        out_specs=pl.BlockSpec((tm, IN_TN), lambda i, j: (i, j)),
        out_shape=jax.ShapeDtypeStruct((m, n_out), F32),
        scratch_shapes=[pltpu.VMEM((tm, D_MODEL), BF16)],
        compiler_params=pltpu.CompilerParams(
            dimension_semantics=("arbitrary", "arbitrary"),
            vmem_limit_bytes=VMEM_LIMIT_BYTES),
        name="in_proj",
    )(x2d, g_pre.reshape(1, D_MODEL), w_bf16, *tables)


def _rope_tables(pos):
    posf = pos.astype(F32)[:, None]
    lane = jnp.arange(HEAD_DIM)

    def build(rot, period):
        half = rot // 2
        inv = ROPE_THETA ** (-jnp.arange(half, dtype=F32) * (2.0 / rot))
        ang = posf * inv[None, :]
        cos, sin = jnp.cos(ang), jnp.sin(ang)
        within = lane % period
        idx = within % half
        first = within < half
        second = jnp.logical_and(within >= half, within < rot)
        cos_l, sin_l = cos[:, idx], sin[:, idx]
        c = jnp.where(jnp.logical_or(first, second)[None, :], cos_l, 1.0)
        s1 = jnp.where(first[None, :], -sin_l, 0.0)
        s2 = jnp.where(second[None, :], sin_l, 0.0)
        return c.astype(F32), s1.astype(F32), s2.astype(F32)

    return build(ROT_A, HEAD_DIM) + build(ROT_B, DIFF_HALF)


def _out_proj_kernel(o_ref, w_ref, g_ref, x_ref, y_ref):
    y = jnp.dot(o_ref[...].astype(BF16), w_ref[...], preferred_element_type=F32)
    ms = jnp.mean(y * y, axis=-1, keepdims=True)
    y_ref[...] = x_ref[...] + y * lax.rsqrt(ms + RMS_EPS) * g_ref[...]


def _out_proj(o2d, w_bf16, g_post, x2d):
    m = x2d.shape[0]
    tm = min(m, OUT_TM_MAX)
    return pl.pallas_call(
        _out_proj_kernel,
        grid=(m // tm,),
        in_specs=[
            pl.BlockSpec((tm, D_MODEL), lambda i: (i, 0)),
            pl.BlockSpec((D_MODEL, D_MODEL), lambda i: (0, 0)),
            pl.BlockSpec((1, D_MODEL), lambda i: (0, 0)),
            pl.BlockSpec((tm, D_MODEL), lambda i: (i, 0)),
        ],
        out_specs=pl.BlockSpec((tm, D_MODEL), lambda i: (i, 0)),
        out_shape=jax.ShapeDtypeStruct((m, D_MODEL), F32),
        compiler_params=pltpu.CompilerParams(
            dimension_semantics=("arbitrary",),
            vmem_limit_bytes=VMEM_LIMIT_BYTES),
        name="out_proj",
    )(o2d, w_bf16, g_post.reshape(1, D_MODEL), x2d)


def _softmax_block(qb, k_blk, v_blk, scale, bias_col, mask, m_ref, l_ref, acc_ref, first):
    s = _nt_dot(qb, k_blk) * scale
    if bias_col is not None:
        s = s + bias_col
    if mask is not None:
        s = jnp.where(mask, s, NEG_INF)
    m_blk = jnp.max(s, axis=1, keepdims=True)
    if first:
        m_new = m_blk
        p = jnp.exp(s - m_new)
        l_ref[...] = jnp.sum(p, axis=1, keepdims=True)
        acc_ref[...] = jnp.dot(p.astype(BF16), v_blk, preferred_element_type=F32)
    else:
        m_old = m_ref[...]
        m_new = jnp.maximum(m_old, m_blk)
        alpha = jnp.exp(m_old - m_new)
        p = jnp.exp(s - m_new)
        l_ref[...] = alpha * l_ref[...] + jnp.sum(p, axis=1, keepdims=True)
        acc_ref[...] = alpha * acc_ref[...] + jnp.dot(p.astype(BF16), v_blk,
                                                      preferred_element_type=F32)
    m_ref[...] = m_new


def _prompt_attn_kernel(lam_ref, q_ref, k_ref, v_ref, gate_ref, gain_ref, o_ref,
                        k_scr, v_scr, kmean_scr, m1, l1, acc1, m2, l2, acc2, *, lam_init):
    head = pl.program_id(1)
    qi = pl.program_id(2)
    n_blk = k_scr.shape[0] // ATT_T

    @pl.when(qi == 0)
    def _():
        kmean_scr[...] = jnp.zeros_like(kmean_scr)
        for j in range(n_blk):
            rows = slice(j * ATT_T, (j + 1) * ATT_T)
            kf = k_ref[rows, :]
            k_scr[rows, :] = kf.astype(BF16)
            v_scr[rows, :] = v_ref[rows, :].astype(BF16)
            kmean_scr[j:j + 1, :] = jnp.mean(kf, axis=0, keepdims=True)

    qf = q_ref[...]
    diag = pl.ds(pl.multiple_of(qi * ATT_T, ATT_T), ATT_T)
    row = lax.broadcasted_iota(jnp.int32, (ATT_T, ATT_T), 0)
    col = lax.broadcasted_iota(jnp.int32, (ATT_T, ATT_T), 1)
    gate = gate_ref[...]
    gate_mul = gate * jax.nn.sigmoid(gate)

    @pl.when(head < HEADS_A)
    def _():
        qb = qf.astype(BF16)
        scale = HEAD_DIM ** -0.5
        km = kmean_scr[...]
        kh, kl = _split2(km)
        qh, ql = _split2(qf)
        g_t = (_nt_dot(kh, qh) + _nt_dot(kh, ql) + _nt_dot(kl, qh))[0:n_blk]
        blk = lax.broadcasted_iota(jnp.int32, (n_blk, ATT_T), 0)
        valid = blk < qi
        g_t = jnp.where(valid, g_t, NEG_INF)
        cnt = jnp.zeros((n_blk, ATT_T), F32)
        for i in range(n_blk):
            gi = g_t[i:i + 1, :]
            beats = jnp.logical_or(gi > g_t, jnp.logical_and(gi == g_t, i < blk))
            cnt = cnt + jnp.where(beats, 1.0, 0.0)
        sel = jnp.logical_and(valid, cnt < float(MOBA_TOPK))
        bias_t = jnp.where(sel, 0.0, NEG_INF)
        bias_full = jnp.concatenate(
            [bias_t, jnp.zeros((HEAD_DIM - n_blk, ATT_T), F32)], axis=0)
        bias = bias_full.T

        _softmax_block(qb, k_scr[diag, :], v_scr[diag, :], scale, None, col <= row,
                       m1, l1, acc1, True)
        for j in range(n_blk - 1):
            @pl.when(j < qi)
            def _():
                rows = slice(j * ATT_T, (j + 1) * ATT_T)
                _softmax_block(qb, k_scr[rows, :], v_scr[rows, :], scale, bias[:, j:j + 1], None,
                               m1, l1, acc1, False)
        o_ref[...] = (acc1[...] / l1[...] * gate_mul).astype(o_ref.dtype)

    @pl.when(jnp.logical_and(head >= HEADS_A, head < B_END))
    def _():
        lane = lax.broadcasted_iota(jnp.int32, (ATT_T, HEAD_DIM), 1)
        q1 = jnp.where(lane < DIFF_HALF, qf, 0.0).astype(BF16)
        q2 = jnp.where(lane >= DIFF_HALF, qf, 0.0).astype(BF16)
        scale = DIFF_HALF ** -0.5
        causal = col <= row
        _softmax_block(q1, k_scr[diag, :], v_scr[diag, :], scale, None, causal, m1, l1, acc1, True)
        _softmax_block(q2, k_scr[diag, :], v_scr[diag, :], scale, None, causal, m2, l2, acc2, True)
        for j in range(n_blk - 1):
            @pl.when(j < qi)
            def _():
                rows = slice(j * ATT_T, (j + 1) * ATT_T)
                _softmax_block(q1, k_scr[rows, :], v_scr[rows, :], scale, None, None,
                               m1, l1, acc1, False)
                _softmax_block(q2, k_scr[rows, :], v_scr[rows, :], scale, None, None,
                               m2, l2, acc2, False)
        d = acc1[...] / l1[...] - lam_ref[0, 0] * (acc2[...] / l2[...])
        ms = jnp.mean(d * d, axis=-1, keepdims=True)
        d = d * lax.rsqrt(ms + RMS_EPS) * gain_ref[...] * (1.0 - lam_init)
        o_ref[...] = (d * gate_mul).astype(o_ref.dtype)

    @pl.when(head >= B_END)
    def _():
        qb = qf.astype(BF16)
        scale = HEAD_DIM ** -0.5
        tri = (row >= col).astype(BF16)

        def sb_block(k_blk, v_blk, earlier, first):
            z = _nt_dot(qb, k_blk) * scale
            sp = _softplus(z)
            log_keep = -sp
            if earlier is not None:
                log_keep = jnp.where(earlier, log_keep, 0.0)
            hi, lo = _split2(log_keep)
            incl = (jnp.dot(hi, tri, preferred_element_type=F32)
                    + jnp.dot(lo, tri, preferred_element_type=F32))
            between = incl - log_keep
            if not first:
                between = between + m1[...]
            a = jnp.exp(z - sp + between)
            if earlier is not None:
                a = jnp.where(earlier, a, 0.0)
            pv = jnp.dot(a.astype(BF16), v_blk, preferred_element_type=F32)
            if first:
                acc1[...] = pv
                m1[...] = incl[:, 0:1]
            else:
                acc1[...] = acc1[...] + pv
                m1[...] = m1[...] + incl[:, 0:1]

        sb_block(k_scr[diag, :], v_scr[diag, :], col < row, True)
        for j in reversed(range(n_blk - 1)):
            @pl.when(j < qi)
            def _():
                rows = slice(j * ATT_T, (j + 1) * ATT_T)
                sb_block(k_scr[rows, :], v_scr[rows, :], None, False)
        o_ref[...] = (acc1[...] * gate_mul).astype(o_ref.dtype)


def _prompt_attention(proj, lam, diff_gain, bsz, t_len, lam_init):
    nq = t_len // ATT_T
    kernel = functools.partial(_prompt_attn_kernel, lam_init=lam_init)
    return pl.pallas_call(
        kernel,
        grid=(bsz, N_HEADS, nq),
        in_specs=[
            pl.BlockSpec(memory_space=pltpu.SMEM),
            pl.BlockSpec((ATT_T, HEAD_DIM), lambda b, h, i: (b * nq + i, h)),
            pl.BlockSpec((t_len, HEAD_DIM), lambda b, h, i: (b, N_HEADS + h)),
            pl.BlockSpec((t_len, HEAD_DIM), lambda b, h, i: (b, 2 * N_HEADS + h)),
            pl.BlockSpec((ATT_T, HEAD_DIM), lambda b, h, i: (b * nq + i, 3 * N_HEADS + h)),
            pl.BlockSpec((1, HEAD_DIM), lambda b, h, i: (0, 0)),
        ],
        out_specs=pl.BlockSpec((ATT_T, HEAD_DIM), lambda b, h, i: (b * nq + i, h)),
        out_shape=jax.ShapeDtypeStruct((bsz * t_len, D_MODEL), BF16),
        scratch_shapes=[
            pltpu.VMEM((t_len, HEAD_DIM), BF16),
            pltpu.VMEM((t_len, HEAD_DIM), BF16),
            pltpu.VMEM((HEAD_DIM, HEAD_DIM), F32),
            pltpu.VMEM((ATT_T, 1), F32), pltpu.VMEM((ATT_T, 1), F32),
            pltpu.VMEM((ATT_T, HEAD_DIM), F32),
            pltpu.VMEM((ATT_T, 1), F32), pltpu.VMEM((ATT_T, 1), F32),
            pltpu.VMEM((ATT_T, HEAD_DIM), F32),
        ],
        compiler_params=pltpu.CompilerParams(
            dimension_semantics=("arbitrary", "arbitrary", "arbitrary"),
            vmem_limit_bytes=VMEM_LIMIT_BYTES),
        name="prompt_attn",
    )(lam.reshape(1, 1), proj, proj, proj, proj, diff_gain.reshape(1, HEAD_DIM))


def _decode_attn_kernel(pt_ref, lam_ref, q_ref, kn_ref, vn_ref, gate_ref, gain_ref, k_ref, v_ref,
                        o_ref, qbd_scr, qbdf_scr, m_scr, l_scr, acc_scr, ksum_scr,
                        mo_m, mo_l, mo_o, *, lam_init, n_pages, page):
    del pt_ref
    p = pl.program_id(1)
    moba_w = HEADS_A * HEAD_DIM
    pages_per_blk = MOBA_BLOCK // page
    n_blk = n_pages // pages_per_blk

    rows = lax.broadcasted_iota(jnp.int32, (DEC_ROWS, 1), 0)
    is_moba = rows < HEADS_A
    is_diff = jnp.logical_or(jnp.logical_and(rows >= HEADS_A, rows < B_END),
                             jnp.logical_and(rows >= N_HEADS + HEADS_A, rows < N_HEADS + B_END))
    is_sb = jnp.logical_and(rows >= B_END, rows < N_HEADS)
    scale_rows = jnp.where(is_diff, DIFF_HALF ** -0.5, HEAD_DIM ** -0.5)

    @pl.when(p == 0)
    def _():
        r2 = lax.broadcasted_iota(jnp.int32, (DEC_ROWS, D_MODEL), 0)
        c2 = lax.broadcasted_iota(jnp.int32, (DEC_ROWS, D_MODEL), 1)
        head_c = c2 >> 7
        second_half = ((c2 >> 6) & 1) == 1
        row_head = jnp.where(r2 < N_HEADS, r2, r2 - N_HEADS)
        row_diff = jnp.logical_and(row_head >= HEADS_A, row_head < B_END)
        own = head_c == row_head
        first_rows = jnp.logical_and(
            r2 < N_HEADS, jnp.logical_or(jnp.logical_not(row_diff), jnp.logical_not(second_half)))
        second_rows = jnp.logical_and(r2 >= N_HEADS, jnp.logical_and(row_diff, second_half))
        qmask = jnp.logical_and(own, jnp.logical_or(first_rows, second_rows))
        qbd = jnp.where(qmask, q_ref[0], 0.0)
        qbdf_scr[...] = qbd
        qbd_scr[...] = qbd.astype(BF16)
        m_scr[...] = jnp.full_like(m_scr, NEG_INF)
        l_scr[...] = jnp.zeros_like(l_scr)
        acc_scr[...] = jnp.zeros_like(acc_scr)

    kb = k_ref[...].astype(BF16)
    vb = v_ref[...].astype(BF16)
    s = _nt_dot(qbd_scr[...], kb) * scale_rows
    ksum_scr[pl.ds(p, 1), :] = jnp.sum(k_ref[:, 0:moba_w], axis=0, keepdims=True)

    m_pg = jnp.max(s, axis=1, keepdims=True)
    m_old = m_scr[...]
    m_new = jnp.maximum(m_old, m_pg)
    p_soft = jnp.exp(s - jnp.where(is_moba, m_pg, m_new))
    l_pg = jnp.sum(p_soft, axis=1, keepdims=True)

    sp = _softplus(s)
    log_keep = -sp
    kr = lax.broadcasted_iota(jnp.int32, (page, page), 0)
    kc = lax.broadcasted_iota(jnp.int32, (page, page), 1)
    tri = (kr >= kc).astype(BF16)
    hi = log_keep.astype(BF16)
    r1 = log_keep - hi.astype(F32)
    mid = r1.astype(BF16)
    lo = (r1 - mid.astype(F32)).astype(BF16)
    incl = (jnp.dot(hi, tri, preferred_element_type=F32)
            + jnp.dot(mid, tri, preferred_element_type=F32)
            + jnp.dot(lo, tri, preferred_element_type=F32))
    a_sb = jnp.exp(s - sp + incl - log_keep)
    total = incl[:, 0:1]

    probs = jnp.where(is_sb, a_sb, p_soft)
    alpha = jnp.where(is_sb, jnp.exp(total), jnp.exp(m_old - m_new))
    pv = jnp.dot(probs.astype(BF16), vb, preferred_element_type=F32)
    acc_scr[...] = alpha * acc_scr[...] + pv
    l_scr[...] = alpha * l_scr[...] + l_pg
    m_scr[...] = m_new
    mo_m[p] = jnp.broadcast_to(m_pg[0:8], (8, HEAD_DIM))
    mo_l[p] = jnp.broadcast_to(l_pg[0:8], (8, HEAD_DIM))
    mo_o[p] = pv[0:8, 0:moba_w]

    @pl.when(p == n_pages - 1)
    def _():
        qf = qbdf_scr[...]
        kn = kn_ref[0]
        vn = vn_ref[0]
        s_new = jnp.sum(qf * kn, axis=1, keepdims=True) * scale_rows
        m_prev = m_scr[...]
        m_fin = jnp.maximum(m_prev, s_new)
        a_fin = jnp.exp(m_prev - m_fin)
        p_new = jnp.exp(s_new - m_fin)
        l_fin = a_fin * l_scr[...] + p_new
        acc = acc_scr[...]
        dn = (a_fin * acc + p_new * vn) / l_fin
        d = dn[0:N_HEADS] - lam_ref[0, 0] * dn[N_HEADS:DEC_ROWS]

        r16 = lax.broadcasted_iota(jnp.int32, (N_HEADS, D_MODEL), 0)
        c16 = lax.broadcasted_iota(jnp.int32, (N_HEADS, D_MODEL), 1)
        own16 = (c16 >> 7) == r16
        ss = jnp.sum(jnp.where(own16, d * d, 0.0), axis=1, keepdims=True)
        d = d * lax.rsqrt(ss * (1.0 / HEAD_DIM) + RMS_EPS) * gain_ref[...] * (1.0 - lam_init)
        diff16 = jnp.logical_and(r16 >= HEADS_A, r16 < B_END)
        val = jnp.where(diff16, d, acc[0:N_HEADS])
        rest = jnp.sum(jnp.where(jnp.logical_and(own16, r16 >= HEADS_A), val, 0.0),
                       axis=0, keepdims=True)

        q8 = qf[0:8, 0:moba_w]
        ks = ksum_scr[...]
        gates = []
        for b in range(n_blk):
            kmean = ks[b * pages_per_blk:b * pages_per_blk + 1]
            for t in range(1, pages_per_blk):
                kmean = kmean + ks[b * pages_per_blk + t:b * pages_per_blk + t + 1]
            kmean = kmean * (1.0 / MOBA_BLOCK)
            gates.append(jnp.sum(q8 * kmean, axis=1, keepdims=True))
        sels = []
        for b in range(n_blk):
            cnt = jnp.zeros((8, 1), F32)
            for i in range(n_blk):
                if i == b:
                    continue
                beats = gates[i] > gates[b]
                if i < b:
                    beats = jnp.logical_or(beats, gates[i] == gates[b])
                cnt = cnt + jnp.where(beats, 1.0, 0.0)
            sels.append(cnt < float(MOBA_TOPK))
        s8 = s_new[0:8]
        m_tot = s8
        for pg in range(n_pages):
            m_tot = jnp.maximum(m_tot, jnp.where(sels[pg // pages_per_blk], mo_m[pg][:, 0:1], NEG_INF))
        w_new = jnp.exp(s8 - m_tot)
        num = w_new * vn[:, 0:moba_w]
        den = w_new
        for pg in range(n_pages):
            w = jnp.where(sels[pg // pages_per_blk], jnp.exp(mo_m[pg][:, 0:1] - m_tot), 0.0)
            num = num + w * mo_o[pg]
            den = den + w * mo_l[pg][:, 0:1]
        om = num / den
        r8 = lax.broadcasted_iota(jnp.int32, (8, moba_w), 0)
        c8 = lax.broadcasted_iota(jnp.int32, (8, moba_w), 1)
        moba_flat = jnp.sum(jnp.where((c8 >> 7) == r8, om, 0.0), axis=0, keepdims=True)

        o_flat = jnp.concatenate([moba_flat, rest[:, moba_w:]], axis=1)
        gate = gate_ref[0]
        o_ref[0] = o_flat * (gate * jax.nn.sigmoid(gate))


def _decode_attention(proj, cache_k, cache_v, page_table, layer, lam, diff_gain, lam_init):
    n_rows = proj.shape[0]
    n_pages = page_table.shape[1]
    page = cache_k.shape[2]
    proj3 = proj.reshape(n_rows, 1, 4 * D_MODEL)
    gain_flat = jnp.tile(diff_gain.reshape(1, HEAD_DIM), (1, N_HEADS))
    kernel = functools.partial(_decode_attn_kernel, lam_init=lam_init, n_pages=n_pages, page=page)
    moba_w = HEADS_A * HEAD_DIM

    def row_spec(section):
        return pl.BlockSpec((1, 1, D_MODEL), lambda b, p, pt: (b, 0, section))

    def page_spec():
        return pl.BlockSpec((None, None, page, D_MODEL),
                            lambda b, p, pt: (layer, pt[b * n_pages + p], 0, 0))

    grid_spec = pltpu.PrefetchScalarGridSpec(
        num_scalar_prefetch=1,
        grid=(n_rows, n_pages),
        in_specs=[
            pl.BlockSpec(memory_space=pltpu.SMEM),
            row_spec(0), row_spec(1), row_spec(2), row_spec(3),
            pl.BlockSpec((1, D_MODEL), lambda b, p, pt: (0, 0)),
            page_spec(), page_spec(),
        ],
        out_specs=pl.BlockSpec((1, 1, D_MODEL), lambda b, p, pt: (b, 0, 0)),
        scratch_shapes=[
            pltpu.VMEM((DEC_ROWS, D_MODEL), BF16),
            pltpu.VMEM((DEC_ROWS, D_MODEL), F32),
            pltpu.VMEM((DEC_ROWS, 1), F32),
            pltpu.VMEM((DEC_ROWS, 1), F32),
            pltpu.VMEM((DEC_ROWS, D_MODEL), F32),
            pltpu.VMEM((n_pages, moba_w), F32),
            pltpu.VMEM((n_pages, 8, HEAD_DIM), F32),
            pltpu.VMEM((n_pages, 8, HEAD_DIM), F32),
            pltpu.VMEM((n_pages, 8, moba_w), F32),
        ],
    )
    out = pl.pallas_call(
        kernel,
        grid_spec=grid_spec,
        out_shape=jax.ShapeDtypeStruct((n_rows, 1, D_MODEL), F32),
        compiler_params=pltpu.CompilerParams(
            dimension_semantics=("arbitrary", "arbitrary"),
            vmem_limit_bytes=VMEM_LIMIT_BYTES),
        name="decode_attn",
    )(page_table.reshape(-1), lam.reshape(1, 1), proj3, proj3, proj3, proj3, gain_flat,
      cache_k, cache_v)
    return out.reshape(n_rows, D_MODEL)


def kernel(x_prompt, x_sample, cache_k, cache_v, page_table, g_pre, w_in, w_out, g_post,
           lam_q1, lam_k1, lam_q2, lam_k2, diff_gain):
    bsz, t_len, _ = x_prompt.shape
    dec_b, dec_t, _ = x_sample.shape
    depth, n_pool, page, _, _ = cache_k.shape
    past_len = page_table.shape[1] * page
    assert dec_t == 1 and t_len % ATT_T == 0 and MOBA_BLOCK % page == 0 and past_len % MOBA_BLOCK == 0

    tab_p = _rope_tables(jnp.arange(t_len, dtype=jnp.int32))
    tab_s = _rope_tables(jnp.full((dec_b,), past_len, dtype=jnp.int32))
    ck = cache_k.reshape(depth, n_pool, page, D_MODEL)
    cv = cache_v.reshape(depth, n_pool, page, D_MODEL)

    y_p = x_prompt.reshape(bsz * t_len, D_MODEL)
    y_s = x_sample.reshape(dec_b, D_MODEL)
    k_p, v_p, k_s, v_s = [], [], [], []
    for layer in range(depth):
        lam_init = 0.8 - 0.6 * math.exp(-0.3 * layer)
        lam = (jnp.exp(jnp.sum(lam_q1[layer].astype(F32) * lam_k1[layer].astype(F32)))
               - jnp.exp(jnp.sum(lam_q2[layer].astype(F32) * lam_k2[layer].astype(F32)))
               + lam_init)
        w_in_b = w_in[layer].astype(BF16)
        w_out_b = w_out[layer].astype(BF16)

        proj_p = _in_proj(y_p, g_pre[layer], w_in_b, tab_p)
        o_p = _prompt_attention(proj_p, lam, diff_gain[layer], bsz, t_len, lam_init)
        y_p = _out_proj(o_p, w_out_b, g_post[layer], y_p)
        k_p.append(proj_p[:, D_MODEL:2 * D_MODEL].reshape(bsz, t_len, N_HEADS, HEAD_DIM))
        v_p.append(proj_p[:, 2 * D_MODEL:3 * D_MODEL].reshape(bsz, t_len, N_HEADS, HEAD_DIM))

        proj_s = _in_proj(y_s, g_pre[layer], w_in_b, tab_s)
        o_s = _decode_attention(proj_s, ck, cv, page_table, layer, lam, diff_gain[layer], lam_init)
        y_s = _out_proj(o_s, w_out_b, g_post[layer], y_s)
        k_s.append(proj_s[:, D_MODEL:2 * D_MODEL].reshape(dec_b, dec_t, N_HEADS, HEAD_DIM))
        v_s.append(proj_s[:, 2 * D_MODEL:3 * D_MODEL].reshape(dec_b, dec_t, N_HEADS, HEAD_DIM))

    return (y_p.reshape(bsz, t_len, D_MODEL), y_s.reshape(dec_b, dec_t, D_MODEL),
            jnp.stack(k_p), jnp.stack(v_p), jnp.stack(k_s), jnp.stack(v_s))
```

```python
import functools
import math

import numpy as np
import jax
import jax.numpy as jnp
from jax import lax
from jax.experimental import pallas as pl
from jax.experimental.pallas import tpu as pltpu

D_MODEL = 2048
HEAD_DIM = 128
N_HEADS = D_MODEL // HEAD_DIM
HEADS_B = (5 * N_HEADS) // 16
HEADS_C = (5 * N_HEADS) // 16
HEADS_A = N_HEADS - HEADS_B - HEADS_C
B_END = HEADS_A + HEADS_B
DIFF_HALF = HEAD_DIM // 2
ROT_A = HEAD_DIM // 4
ROT_B = DIFF_HALF // 4
ROPE_THETA = 500000.0
MOBA_BLOCK = 256
MOBA_TOPK = 3
RMS_EPS = 1e-6
NEG_INF = -1e30

F32 = jnp.float32
BF16 = jnp.bfloat16

VMEM_LIMIT_BYTES = 56 * 1024 * 1024
SUBLANES = 8

IN_TN = 512
IN_TM_MAX = 1024
OUT_TM_MAX = 512
ATT_T = MOBA_BLOCK
DEC_PAGES_PER_STEP = 4
DEC_ROWS = 4 * SUBLANES
DEC_ROW_HEAD = np.array(
    [min(i, N_HEADS - 1) for i in range(SUBLANES)]
    + [min(HEADS_A + i, N_HEADS - 1) for i in range(SUBLANES)] * 2
    + [min(B_END + i, N_HEADS - 1) for i in range(SUBLANES)], dtype=np.int32)
DEC_ROW_VALID = np.array(
    [i < HEADS_A for i in range(SUBLANES)]
    + [i < HEADS_B for i in range(SUBLANES)] * 2
    + [i < HEADS_C for i in range(SUBLANES)])


def _split2(x):
    hi = x.astype(BF16)
    lo = (x - hi.astype(F32)).astype(BF16)
    return hi, lo


def _softplus(z):
    return jnp.maximum(z, 0.0) + jnp.log1p(jnp.exp(-jnp.abs(z)))


def _mm(a, b):
    return jnp.dot(a, b, preferred_element_type=F32)


def _nt_dot(a, b):
    return lax.dot_general(a, b, (((1,), (1,)), ((), ())), preferred_element_type=F32)


def _rope_apply(x, c_ref, s1_ref, s2_ref, half):
    return (x * c_ref[...]
            + pltpu.roll(x, HEAD_DIM - half, axis=1) * s1_ref[...]
            + pltpu.roll(x, half, axis=1) * s2_ref[...])


def _in_proj_kernel(x_ref, g_ref, w_ref, ca_ref, s1a_ref, s2a_ref, cb_ref, s1b_ref, s2b_ref,
                    o_ref, h_scr):
    j = pl.program_id(1)

    @pl.when(j == 0)
    def _():
        xf = x_ref[...]
        ms = jnp.mean(xf * xf, axis=-1, keepdims=True)
        h_scr[...] = (xf * lax.rsqrt(ms + RMS_EPS) * g_ref[...]).astype(BF16)

    acc = _mm(h_scr[...], w_ref[...])
    tiles_per_section = D_MODEL // IN_TN
    heads_per_tile = IN_TN // HEAD_DIM
    section = j // tiles_per_section
    head0 = (j % tiles_per_section) * heads_per_tile
    rotated = section < 2
    for c in range(heads_per_tile):
        head = head0 + c
        sl = slice(c * HEAD_DIM, (c + 1) * HEAD_DIM)
        xa = acc[:, sl]
        is_a = jnp.logical_and(rotated, head < HEADS_A)
        is_b = jnp.logical_and(rotated, jnp.logical_and(head >= HEADS_A, head < B_END))

        @pl.when(is_a)
        def _():
            o_ref[:, sl] = _rope_apply(xa, ca_ref, s1a_ref, s2a_ref, ROT_A // 2)

        @pl.when(is_b)
        def _():
            o_ref[:, sl] = _rope_apply(xa, cb_ref, s1b_ref, s2b_ref, ROT_B // 2)

        @pl.when(jnp.logical_not(jnp.logical_or(is_a, is_b)))
        def _():
            o_ref[:, sl] = xa


def _in_proj(x2d, g_pre, w_bf16, tables):
    m = x2d.shape[0]
    tm = min(m, IN_TM_MAX)
    n_out = w_bf16.shape[1]
    tab_rows = tables[0].shape[0]
    tab_blocks = tab_rows // tm
    tab_spec = pl.BlockSpec((tm, HEAD_DIM), lambda i, j: (i % tab_blocks, 0))
    return pl.pallas_call(
        _in_proj_kernel,
        grid=(m // tm, n_out // IN_TN),
        in_specs=[
            pl.BlockSpec((tm, D_MODEL), lambda i, j: (i, 0)),
            pl.BlockSpec((1, D_MODEL), lambda i, j: (0, 0)),
            pl.BlockSpec((D_MODEL, IN_TN), lambda i, j: (0, j)),
        ] + [tab_spec] * 6,
        out_specs=pl.BlockSpec((tm, IN_TN), lambda i, j: (i, j)),
        out_shape=jax.ShapeDtypeStruct((m, n_out), F32),
        scratch_shapes=[pltpu.VMEM((tm, D_MODEL), BF16)],
        compiler_params=pltpu.CompilerParams(
            dimension_semantics=("arbitrary", "arbitrary"),
            vmem_limit_bytes=VMEM_LIMIT_BYTES),
        name="in_proj",
    )(x2d, g_pre.reshape(1, D_MODEL), w_bf16, *tables)


def _rope_tables(pos):
    posf = pos.astype(F32)[:, None]
    lane = jnp.arange(HEAD_DIM)

    def build(rot, period):
        half = rot // 2
        inv = ROPE_THETA ** (-jnp.arange(half, dtype=F32) * (2.0 / rot))
        ang = posf * inv[None, :]
        cos, sin = jnp.cos(ang), jnp.sin(ang)
        within = lane % period
        idx = within % half
        first = within < half
        second = jnp.logical_and(within >= half, within < rot)
        cos_l, sin_l = cos[:, idx], sin[:, idx]
        c = jnp.where(jnp.logical_or(first, second)[None, :], cos_l, 1.0)
        s1 = jnp.where(first[None, :], -sin_l, 0.0)
        s2 = jnp.where(second[None, :], sin_l, 0.0)
        return c.astype(F32), s1.astype(F32), s2.astype(F32)

    return build(ROT_A, HEAD_DIM) + build(ROT_B, DIFF_HALF)


def _out_proj_kernel(o_ref, w_ref, g_ref, x_ref, y_ref):
    y = _mm(o_ref[...].astype(BF16), w_ref[...])
    ms = jnp.mean(y * y, axis=-1, keepdims=True)
    y_ref[...] = x_ref[...] + y * lax.rsqrt(ms + RMS_EPS) * g_ref[...]


def _out_proj(o2d, w_bf16, g_post, x2d):
    m = x2d.shape[0]
    tm = min(m, OUT_TM_MAX)
    return pl.pallas_call(
        _out_proj_kernel,
        grid=(m // tm,),
        in_specs=[
            pl.BlockSpec((tm, D_MODEL), lambda i: (i, 0)),
            pl.BlockSpec((D_MODEL, D_MODEL), lambda i: (0, 0)),
            pl.BlockSpec((1, D_MODEL), lambda i: (0, 0)),
            pl.BlockSpec((tm, D_MODEL), lambda i: (i, 0)),
        ],
        out_specs=pl.BlockSpec((tm, D_MODEL), lambda i: (i, 0)),
        out_shape=jax.ShapeDtypeStruct((m, D_MODEL), F32),
        compiler_params=pltpu.CompilerParams(
            dimension_semantics=("arbitrary",),
            vmem_limit_bytes=VMEM_LIMIT_BYTES),
        name="out_proj",
    )(o2d, w_bf16, g_post.reshape(1, D_MODEL), x2d)


def _softmax_block_t(k_blk, vt_blk, qt, scale, bias_row, mask, m_ref, l_ref, acc_ref, b, first):
    s = _mm(k_blk, qt) * scale
    if bias_row is not None:
        s = s + bias_row
    if mask is not None:
        s = jnp.where(mask, s, NEG_INF)
    m_blk = jnp.max(s, axis=0, keepdims=True)
    if first:
        p = jnp.exp(s - m_blk)
        m_ref[b] = m_blk
        l_ref[b] = jnp.sum(p, axis=0, keepdims=True)
        acc_ref[b] = _mm(vt_blk, p.astype(BF16))
    else:
        m_old = m_ref[b]
        m_new = jnp.maximum(m_old, m_blk)
        alpha = jnp.exp(m_old - m_new)
        p = jnp.exp(s - m_new)
        m_ref[b] = m_new
        l_ref[b] = alpha * l_ref[b] + jnp.sum(p, axis=0, keepdims=True)
        acc_ref[b] = alpha * acc_ref[b] + _mm(vt_blk, p.astype(BF16))


def _prompt_attn_kernel(lam_ref, q_ref, k_ref, v_ref, gate_ref, gain_ref, o_ref,
                        k_scr, vt_scr, kmean_scr, tri_scr, qt1_scr, qt2_scr, bias_scr,
                        m1, l1, acc1, m2, l2, acc2, *, lam_init):
    head = pl.program_id(0)
    qi = pl.program_id(1)
    nb = q_ref.shape[0]
    n_blk = k_scr.shape[1] // ATT_T
    kidx = lax.broadcasted_iota(jnp.int32, (ATT_T, ATT_T), 0)
    qidx = lax.broadcasted_iota(jnp.int32, (ATT_T, ATT_T), 1)

    @pl.when(qi == 0)
    def _():
        tri_scr[...] = (qidx >= kidx).astype(BF16)
        kmean_scr[...] = jnp.zeros_like(kmean_scr)
        for b in range(nb):
            for j in range(n_blk):
                rows = slice(j * ATT_T, (j + 1) * ATT_T)
                kf = k_ref[b, rows, :]
                k_scr[b, rows, :] = kf.astype(BF16)
                vt_scr[b, j] = v_ref[b, rows, :].T.astype(BF16)
                kmean_scr[b, j:j + 1, :] = jnp.mean(kf, axis=0, keepdims=True)

    diag = pl.ds(pl.multiple_of(qi * ATT_T, ATT_T), ATT_T)

    def blk_rows(j):
        return pl.ds(pl.multiple_of(j * ATT_T, ATT_T), ATT_T)

    def gate_mul(b):
        g = gate_ref[b]
        return g * jax.nn.sigmoid(g)

    @pl.when(head < HEADS_A)
    def _():
        scale = HEAD_DIM ** -0.5
        for b in range(nb):
            qt = q_ref[b].T
            qt1_scr[b] = qt.astype(BF16)
            kh, kl = _split2(kmean_scr[b])
            qh, ql = _split2(qt)
            g_t = (_mm(kh, qh) + _mm(kh, ql) + _mm(kl, qh))[0:n_blk]
            blk = lax.broadcasted_iota(jnp.int32, (n_blk, ATT_T), 0)
            valid = blk < qi
            g_t = jnp.where(valid, g_t, NEG_INF)
            cnt = jnp.zeros((n_blk, ATT_T), F32)
            for i in range(n_blk):
                gi = g_t[i:i + 1, :]
                beats = jnp.logical_or(gi > g_t, jnp.logical_and(gi == g_t, i < blk))
                cnt = cnt + jnp.where(beats, 1.0, 0.0)
            sel = jnp.logical_and(valid, cnt < float(MOBA_TOPK))
            bias_scr[b] = jnp.where(sel, 0.0, NEG_INF)
        for b in range(nb):
            _softmax_block_t(k_scr[b, diag, :], vt_scr[b, qi], qt1_scr[b], scale, None,
                             kidx <= qidx, m1, l1, acc1, b, True)

        def body(j, carry):
            for b in range(nb):
                _softmax_block_t(k_scr[b, blk_rows(j), :], vt_scr[b, j], qt1_scr[b], scale,
                                 bias_scr[b, pl.ds(j, 1), :], None, m1, l1, acc1, b, False)
            return carry

        lax.fori_loop(0, qi, body, 0)
        for b in range(nb):
            o_t = acc1[b] / l1[b]
            o_ref[b] = (o_t.T * gate_mul(b)).astype(o_ref.dtype)

    @pl.when(jnp.logical_and(head >= HEADS_A, head < B_END))
    def _():
        scale = DIFF_HALF ** -0.5
        dim = lax.broadcasted_iota(jnp.int32, (HEAD_DIM, ATT_T), 0)
        for b in range(nb):
            qt = q_ref[b].T
            qt1_scr[b] = jnp.where(dim < DIFF_HALF, qt, 0.0).astype(BF16)
            qt2_scr[b] = jnp.where(dim >= DIFF_HALF, qt, 0.0).astype(BF16)
        causal = kidx <= qidx
        for b in range(nb):
            _softmax_block_t(k_scr[b, diag, :], vt_scr[b, qi], qt1_scr[b], scale, None, causal,
                             m1, l1, acc1, b, True)
            _softmax_block_t(k_scr[b, diag, :], vt_scr[b, qi], qt2_scr[b], scale, None, causal,
                             m2, l2, acc2, b, True)

        def body(j, carry):
            for b in range(nb):
                k_blk = k_scr[b, blk_rows(j), :]
                vt_blk = vt_scr[b, j]
                _softmax_block_t(k_blk, vt_blk, qt1_scr[b], scale, None, None, m1, l1, acc1, b, False)
                _softmax_block_t(k_blk, vt_blk, qt2_scr[b], scale, None, None, m2, l2, acc2, b, False)
            return carry

        lax.fori_loop(0, qi, body, 0)
        for b in range(nb):
            d = acc1[b] / l1[b] - lam_ref[0, 0] * (acc2[b] / l2[b])
            ms = jnp.mean(d * d, axis=0, keepdims=True)
            d = (d * lax.rsqrt(ms + RMS_EPS)).T * gain_ref[...] * (1.0 - lam_init)
            o_ref[b] = (d * gate_mul(b)).astype(o_ref.dtype)

    @pl.when(head >= B_END)
    def _():
        scale = HEAD_DIM ** -0.5
        for b in range(nb):
            qt1_scr[b] = q_ref[b].T.astype(BF16)

        def sb_block(b, k_blk, vt_blk, earlier, first):
            z = _mm(k_blk, qt1_scr[b]) * scale
            sp = _softplus(z)
            log_keep = -sp
            if earlier is not None:
                log_keep = jnp.where(earlier, log_keep, 0.0)
            hi, lo = _split2(log_keep)
            tri = tri_scr[...]
            incl = _mm(tri, hi) + _mm(tri, lo)
            between = incl - log_keep
            if not first:
                between = between + m1[b]
            a = jnp.exp(z - sp + between)
            if earlier is not None:
                a = jnp.where(earlier, a, 0.0)
            pv = _mm(vt_blk, a.astype(BF16))
            if first:
                acc1[b] = pv
                m1[b] = incl[0:1, :]
            else:
                acc1[b] = acc1[b] + pv
                m1[b] = m1[b] + incl[0:1, :]

        for b in range(nb):
            sb_block(b, k_scr[b, diag, :], vt_scr[b, qi], kidx < qidx, True)

        def body(t, carry):
            j = qi - 1 - t
            for b in range(nb):
                sb_block(b, k_scr[b, blk_rows(j), :], vt_scr[b, j], None, False)
            return carry

        lax.fori_loop(0, qi, body, 0)
        for b in range(nb):
            o_ref[b] = (acc1[b].T * gate_mul(b)).astype(o_ref.dtype)


def _prompt_attention(proj, lam, diff_gain, bsz, t_len, lam_init):
    nq = t_len // ATT_T
    proj3 = proj.reshape(bsz, t_len, 4 * D_MODEL)
    kernel = functools.partial(_prompt_attn_kernel, lam_init=lam_init)
    stat = pltpu.VMEM((bsz, 1, ATT_T), F32)
    acc = pltpu.VMEM((bsz, HEAD_DIM, ATT_T), F32)
    out = pl.pallas_call(
        kernel,
        grid=(N_HEADS, nq),
        in_specs=[
            pl.BlockSpec(memory_space=pltpu.SMEM),
            pl.BlockSpec((bsz, ATT_T, HEAD_DIM), lambda h, i: (0, i, h)),
            pl.BlockSpec((bsz, t_len, HEAD_DIM), lambda h, i: (0, 0, N_HEADS + h)),
            pl.BlockSpec((bsz, t_len, HEAD_DIM), lambda h, i: (0, 0, 2 * N_HEADS + h)),
            pl.BlockSpec((bsz, ATT_T, HEAD_DIM), lambda h, i: (0, i, 3 * N_HEADS + h)),
            pl.BlockSpec((1, HEAD_DIM), lambda h, i: (0, 0)),
        ],
        out_specs=pl.BlockSpec((bsz, ATT_T, HEAD_DIM), lambda h, i: (0, i, h)),
        out_shape=jax.ShapeDtypeStruct((bsz, t_len, D_MODEL), BF16),
        scratch_shapes=[
            pltpu.VMEM((bsz, t_len, HEAD_DIM), BF16),
            pltpu.VMEM((bsz, nq, HEAD_DIM, ATT_T), BF16),
            pltpu.VMEM((bsz, 2 * SUBLANES, HEAD_DIM), F32),
            pltpu.VMEM((ATT_T, ATT_T), BF16),
            pltpu.VMEM((bsz, HEAD_DIM, ATT_T), BF16),
            pltpu.VMEM((bsz, HEAD_DIM, ATT_T), BF16),
            pltpu.VMEM((bsz, nq, ATT_T), F32),
            stat, stat, acc, stat, stat, acc,
        ],
        compiler_params=pltpu.CompilerParams(
            dimension_semantics=("arbitrary", "arbitrary"),
            vmem_limit_bytes=VMEM_LIMIT_BYTES),
        name="prompt_attn",
    )(lam.reshape(1, 1), proj3, proj3, proj3, proj3, diff_gain.reshape(1, HEAD_DIM))
    return out.reshape(bsz * t_len, D_MODEL)


def _decode_attn_kernel(pt_ref, lam_ref, q_ref, kn_ref, vn_ref, gate_ref, gain_ref, *rest,
                        lam_init, n_steps, page):
    del pt_ref
    pps = DEC_PAGES_PER_STEP
    k_refs = rest[0:pps]
    v_refs = rest[pps:2 * pps]
    (o_ref, qs_scr, qblk_scr, bias_scr, m_scr, l_scr, acc_scr, ksum_scr,
     mo_m, mo_l, mo_o) = rest[2 * pps:]
    step = pl.program_id(1)
    n_pages = n_steps * pps
    lanes = page * N_HEADS
    n_tiles = lanes // HEAD_DIM
    pages_per_blk = MOBA_BLOCK // page
    n_blk = n_pages // pages_per_blk
    T0, T1, T2, T3 = (slice(i * SUBLANES, (i + 1) * SUBLANES) for i in range(4))

    rows = lax.broadcasted_iota(jnp.int32, (DEC_ROWS, 1), 0)
    tile_of_row = rows >> 3
    row_head = jnp.where(tile_of_row == 0, rows,
                         jnp.where(tile_of_row == 3, B_END + (rows & 7), HEADS_A + (rows & 7)))

    @pl.when(step == 0)
    def _():
        is_diff = jnp.logical_or(tile_of_row == 1, tile_of_row == 2)
        qs = q_ref[...] * jnp.where(is_diff, DIFF_HALF ** -0.5, HEAD_DIM ** -0.5)
        qs_scr[...] = qs
        zero = jnp.zeros_like(qs)
        qblk_scr[...] = jnp.concatenate(
            [jnp.concatenate([qs, zero], axis=1), jnp.concatenate([zero, qs], axis=1)],
            axis=0).astype(BF16)
        c2 = lax.broadcasted_iota(jnp.int32, (DEC_ROWS, lanes), 1)
        bias_scr[...] = jnp.where((c2 & (N_HEADS - 1)) == row_head, 0.0, NEG_INF)
        m_scr[...] = jnp.full_like(m_scr, NEG_INF)
        l_scr[...] = jnp.zeros_like(l_scr)
        acc_scr[...] = jnp.zeros_like(acc_scr)

    qblk = qblk_scr[...]
    s_pages = []
    for t in range(0, pps, 2):
        k2 = jnp.concatenate([k_refs[t][...].astype(BF16), k_refs[t + 1][...].astype(BF16)], axis=1)
        s2 = _nt_dot(qblk, k2)
        s_pages += [s2[0:DEC_ROWS], s2[DEC_ROWS:2 * DEC_ROWS]]
    for t in range(pps):
        k3 = k_refs[t][...].reshape(page, N_HEADS, HEAD_DIM)
        ksum_scr[step * pps + t] = jnp.sum(k3[:, 0:SUBLANES, :], axis=0)

    bias = bias_scr[...]
    p0, m0, l0 = [], [], []
    for t in range(pps):
        s0 = s_pages[t][T0] + bias[T0]
        m_pg = jnp.max(s0, axis=1, keepdims=True)
        p_pg = jnp.exp(s0 - m_pg)
        p0.append(p_pg)
        m0.append(m_pg)
        l0.append(jnp.sum(p_pg, axis=1, keepdims=True))
    dsl = slice(SUBLANES, 3 * SUBLANES)
    sd = [s_pages[t][dsl] + bias[dsl] for t in range(pps)]
    m_old = m_scr[...]
    m_new = m_old
    for t in range(pps):
        m_new = jnp.maximum(m_new, jnp.max(sd[t], axis=1, keepdims=True))
    alpha_d = jnp.exp(m_old - m_new)
    pd = [jnp.exp(sd[t] - m_new) for t in range(pps)]
    l_new = alpha_d * l_scr[...]
    for t in range(pps):
        l_new = l_new + jnp.sum(pd[t], axis=1, keepdims=True)
    m_scr[...] = m_new
    l_scr[...] = l_new

    lane_t = lax.broadcasted_iota(jnp.int32, (SUBLANES, HEAD_DIM), 1)
    row_t = lax.broadcasted_iota(jnp.int32, (SUBLANES, HEAD_DIM), 0)
    own3 = (lane_t & (N_HEADS - 1)) == B_END + row_t
    z_tiles, sp_tiles, lk_tiles = [], [], []
    for t in range(pps):
        z = s_pages[t][T3]
        sp = _softplus(z)
        for i in range(n_tiles):
            ls = slice(i * HEAD_DIM, (i + 1) * HEAD_DIM)
            z_tiles.append(z[:, ls])
            sp_tiles.append(sp[:, ls])
            lk_tiles.append(jnp.where(own3, -sp[:, ls], 0.0))
    lk_all = jnp.concatenate(lk_tiles, axis=0)
    hi, lo = _split2(lk_all)
    kr = lax.broadcasted_iota(jnp.int32, (HEAD_DIM, HEAD_DIM), 0)
    kc = lax.broadcasted_iota(jnp.int32, (HEAD_DIM, HEAD_DIM), 1)
    tri = (kr >= kc).astype(BF16)
    incl_all = _mm(hi, tri) + _mm(lo, tri)
    later = jnp.zeros((SUBLANES, 1), F32)
    a_tiles = [None] * (pps * n_tiles)
    for idx in reversed(range(pps * n_tiles)):
        incl = incl_all[idx * SUBLANES:(idx + 1) * SUBLANES]
        between = incl - lk_tiles[idx] + later
        a_tiles[idx] = jnp.where(own3, jnp.exp(z_tiles[idx] - sp_tiles[idx] + between), 0.0)
        later = later + incl[:, 0:1]
    alpha = jnp.concatenate([alpha_d, jnp.exp(later)], axis=0)

    pv_rest = jnp.zeros((3 * SUBLANES, HEAD_DIM), F32)
    for t in range(pps):
        a_pg = jnp.concatenate(a_tiles[t * n_tiles:(t + 1) * n_tiles], axis=1)
        probs = jnp.concatenate([p0[t], pd[t], a_pg], axis=0).astype(BF16)
        pv = _mm(probs, v_refs[t][...].astype(BF16))
        pg = step * pps + t
        mo_m[pg] = jnp.broadcast_to(m0[t], (SUBLANES, HEAD_DIM))
        mo_l[pg] = jnp.broadcast_to(l0[t], (SUBLANES, HEAD_DIM))
        mo_o[pg] = pv[T0]
        pv_rest = pv_rest + pv[SUBLANES:DEC_ROWS]
    acc_scr[...] = alpha * acc_scr[...] + pv_rest

    @pl.when(step == n_steps - 1)
    def _():
        qs = qs_scr[...]
        vn = vn_ref[...]
        s_new = jnp.sum(qs * kn_ref[...], axis=1, keepdims=True)

        m_prev = m_scr[...]
        s_d = s_new[dsl]
        m_fin = jnp.maximum(m_prev, s_d)
        a_fin = jnp.exp(m_prev - m_fin)
        p_new = jnp.exp(s_d - m_fin)
        l_fin = a_fin * l_scr[...] + p_new
        acc = acc_scr[...]
        dn = (a_fin * acc[0:2 * SUBLANES] + p_new * vn[dsl]) / l_fin
        d = dn[0:SUBLANES] - lam_ref[0, 0] * dn[SUBLANES:2 * SUBLANES]
        ms = jnp.mean(d * d, axis=1, keepdims=True)
        d = d * lax.rsqrt(ms + RMS_EPS) * gain_ref[...] * (1.0 - lam_init)

        q8 = q_ref[T0, :]
        gates = []
        for blk in range(n_blk):
            ksum = ksum_scr[blk * pages_per_blk]
            for u in range(1, pages_per_blk):
                ksum = ksum + ksum_scr[blk * pages_per_blk + u]
            gates.append(jnp.sum(q8 * (ksum * (1.0 / MOBA_BLOCK)), axis=1, keepdims=True))
        sels = []
        for blk in range(n_blk):
            cnt = jnp.zeros((SUBLANES, 1), F32)
            for i in range(n_blk):
                if i == blk:
                    continue
                beats = gates[i] > gates[blk]
                if i < blk:
                    beats = jnp.logical_or(beats, gates[i] == gates[blk])
                cnt = cnt + jnp.where(beats, 1.0, 0.0)
            sels.append(cnt < float(MOBA_TOPK))
        s8 = s_new[T0]
        m_tot = s8
        for pg in range(n_pages):
            m_tot = jnp.maximum(
                m_tot, jnp.where(sels[pg // pages_per_blk], mo_m[pg][:, 0:1], NEG_INF))
        w_new = jnp.exp(s8 - m_tot)
        num = w_new * vn[T0]
        den = w_new
        for pg in range(n_pages):
            w = jnp.where(sels[pg // pages_per_blk], jnp.exp(mo_m[pg][:, 0:1] - m_tot), 0.0)
            num = num + w * mo_o[pg]
            den = den + w * mo_l[pg][:, 0:1]
        om = num / den

        o_rows = jnp.concatenate(
            [om, d, jnp.zeros((SUBLANES, HEAD_DIM), F32), acc[2 * SUBLANES:3 * SUBLANES]], axis=0)
        gate = gate_ref[...]
        o_ref[...] = o_rows * (gate * jax.nn.sigmoid(gate))


def _decode_attention(proj, cache_k, cache_v, page_table, layer, lam, diff_gain, lam_init):
    n_rows = proj.shape[0]
    n_pages = page_table.shape[1]
    depth, n_pool, page = cache_k.shape[0:3]
    pps = DEC_PAGES_PER_STEP
    n_steps = n_pages // pps
    lanes = page * N_HEADS
    ck = cache_k.reshape(depth, n_pool, lanes, HEAD_DIM)
    cv = cache_v.reshape(depth, n_pool, lanes, HEAD_DIM)

    sections = proj.reshape(n_rows, 4, N_HEADS, HEAD_DIM)[:, :, DEC_ROW_HEAD, :]
    dim = np.arange(HEAD_DIM)
    tile = np.arange(DEC_ROWS) // SUBLANES
    qmask = DEC_ROW_VALID[:, None] & np.where(
        tile[:, None] == 1, dim[None, :] < DIFF_HALF,
        np.where(tile[:, None] == 2, dim[None, :] >= DIFF_HALF, True))
    q32 = jnp.where(jnp.asarray(qmask)[None], sections[:, 0], 0.0)
    kn32, vn32, gate32 = sections[:, 1], sections[:, 2], sections[:, 3]

    kernel = functools.partial(_decode_attn_kernel, lam_init=lam_init, n_steps=n_steps, page=page)

    def row_spec():
        return pl.BlockSpec((None, DEC_ROWS, HEAD_DIM), lambda b, s, pt: (b, 0, 0))

    def page_spec(t):
        return pl.BlockSpec((None, None, lanes, HEAD_DIM),
                            lambda b, s, pt: (layer, pt[b * n_pages + s * pps + t], 0, 0))

    grid_spec = pltpu.PrefetchScalarGridSpec(
        num_scalar_prefetch=1,
        grid=(n_rows, n_steps),
        in_specs=[pl.BlockSpec(memory_space=pltpu.SMEM),
                  row_spec(), row_spec(), row_spec(), row_spec(),
                  pl.BlockSpec((1, HEAD_DIM), lambda b, s, pt: (0, 0))]
        + [page_spec(t) for t in range(pps)] * 2,
        out_specs=row_spec(),
        scratch_shapes=[
            pltpu.VMEM((DEC_ROWS, HEAD_DIM), F32),
            pltpu.VMEM((2 * DEC_ROWS, 2 * HEAD_DIM), BF16),
            pltpu.VMEM((DEC_ROWS, lanes), F32),
            pltpu.VMEM((2 * SUBLANES, 1), F32),
            pltpu.VMEM((2 * SUBLANES, 1), F32),
            pltpu.VMEM((3 * SUBLANES, HEAD_DIM), F32),
            pltpu.VMEM((n_pages, SUBLANES, HEAD_DIM), F32),
            pltpu.VMEM((n_pages, SUBLANES, HEAD_DIM), F32),
            pltpu.VMEM((n_pages, SUBLANES, HEAD_DIM), F32),
            pltpu.VMEM((n_pages, SUBLANES, HEAD_DIM), F32),
        ],
    )
    out = pl.pallas_call(
        kernel,
        grid_spec=grid_spec,
        out_shape=jax.ShapeDtypeStruct((n_rows, DEC_ROWS, HEAD_DIM), F32),
        compiler_params=pltpu.CompilerParams(
            dimension_semantics=("arbitrary", "arbitrary"),
            vmem_limit_bytes=VMEM_LIMIT_BYTES),
        name="decode_attn",
    )(page_table.reshape(-1), lam.reshape(1, 1), q32, kn32, vn32, gate32,
      diff_gain.reshape(1, HEAD_DIM), *([ck] * pps), *([cv] * pps))
    heads = jnp.concatenate(
        [out[:, 0:HEADS_A], out[:, SUBLANES:SUBLANES + HEADS_B],
         out[:, 3 * SUBLANES:3 * SUBLANES + HEADS_C]], axis=1)
    return heads.reshape(n_rows, D_MODEL)


def kernel(x_prompt, x_sample, cache_k, cache_v, page_table, g_pre, w_in, w_out, g_post,
           lam_q1, lam_k1, lam_q2, lam_k2, diff_gain):
    bsz, t_len, _ = x_prompt.shape
    dec_b, dec_t, _ = x_sample.shape
    depth, _, page, _, _ = cache_k.shape
    n_pages = page_table.shape[1]
    past_len = n_pages * page
    assert dec_t == 1 and t_len % ATT_T == 0 and MOBA_BLOCK % page == 0
    assert past_len % MOBA_BLOCK == 0 and n_pages % DEC_PAGES_PER_STEP == 0
    assert (page * N_HEADS) % (2 * HEAD_DIM) == 0

    tab_p = _rope_tables(jnp.arange(t_len, dtype=jnp.int32))
    tab_s = _rope_tables(jnp.full((dec_b,), past_len, dtype=jnp.int32))

    y_p = x_prompt.reshape(bsz * t_len, D_MODEL)
    y_s = x_sample.reshape(dec_b, D_MODEL)
    k_p, v_p, k_s, v_s = [], [], [], []
    for layer in range(depth):
        lam_init = 0.8 - 0.6 * math.exp(-0.3 * layer)
        lam = (jnp.exp(jnp.sum(lam_q1[layer].astype(F32) * lam_k1[layer].astype(F32)))
               - jnp.exp(jnp.sum(lam_q2[layer].astype(F32) * lam_k2[layer].astype(F32)))
               + lam_init)
        w_in_b = w_in[layer].astype(BF16)
        w_out_b = w_out[layer].astype(BF16)

        proj_p = _in_proj(y_p, g_pre[layer], w_in_b, tab_p)
        o_p = _prompt_attention(proj_p, lam, diff_gain[layer], bsz, t_len, lam_init)
        y_p = _out_proj(o_p, w_out_b, g_post[layer], y_p)
        k_p.append(proj_p[:, D_MODEL:2 * D_MODEL].reshape(bsz, t_len, N_HEADS, HEAD_DIM))
        v_p.append(proj_p[:, 2 * D_MODEL:3 * D_MODEL].reshape(bsz, t_len, N_HEADS, HEAD_DIM))

        proj_s = _in_proj(y_s, g_pre[layer], w_in_b, tab_s)
        o_s = _decode_attention(proj_s, cache_k, cache_v, page_table, layer, lam,
                                diff_gain[layer], lam_init)
        y_s = _out_proj(o_s, w_out_b, g_post[layer], y_s)
        k_s.append(proj_s[:, D_MODEL:2 * D_MODEL].reshape(dec_b, dec_t, N_HEADS, HEAD_DIM))
        v_s.append(proj_s[:, 2 * D_MODEL:3 * D_MODEL].reshape(dec_b, dec_t, N_HEADS, HEAD_DIM))

    return (y_p.reshape(bsz, t_len, D_MODEL), y_s.reshape(dec_b, dec_t, D_MODEL),
            jnp.stack(k_p), jnp.stack(v_p), jnp.stack(k_s), jnp.stack(v_s))
```

```python
import functools
import math

import numpy as np
import jax
import jax.numpy as jnp
from jax import lax
from jax.experimental import pallas as pl
from jax.experimental.pallas import tpu as pltpu

D_MODEL = 2048
HEAD_DIM = 128
N_HEADS = D_MODEL // HEAD_DIM
HEADS_B = (5 * N_HEADS) // 16
HEADS_C = (5 * N_HEADS) // 16
HEADS_A = N_HEADS - HEADS_B - HEADS_C
B_END = HEADS_A + HEADS_B
DIFF_HALF = HEAD_DIM // 2
ROT_A = HEAD_DIM // 4
ROT_B = DIFF_HALF // 4
ROPE_THETA = 500000.0
MOBA_BLOCK = 256
MOBA_TOPK = 3
RMS_EPS = 1e-6
NEG_INF = -1e30

F32 = jnp.float32
BF16 = jnp.bfloat16

VMEM_LIMIT_BYTES = 56 * 1024 * 1024
SUBLANES = 8

IN_TN = 512
IN_TM_MAX = 1024
OUT_TM_MAX = 512
ATT_T = MOBA_BLOCK
MOBA_GROUP = 4
DIFF_GROUP = 4
SB_GROUP = 4
DEC_PAGES_PER_STEP = 4
DEC_ROWS = 4 * SUBLANES
DEC_ROW_HEAD = np.array(
    [min(i, N_HEADS - 1) for i in range(SUBLANES)]
    + [min(HEADS_A + i, N_HEADS - 1) for i in range(SUBLANES)] * 2
    + [min(B_END + i, N_HEADS - 1) for i in range(SUBLANES)], dtype=np.int32)
DEC_ROW_VALID = np.array(
    [i < HEADS_A for i in range(SUBLANES)]
    + [i < HEADS_B for i in range(SUBLANES)] * 2
    + [i < HEADS_C for i in range(SUBLANES)])


def _split2(x):
    hi = x.astype(BF16)
    lo = (x - hi.astype(F32)).astype(BF16)
    return hi, lo


def _softplus(z):
    return jnp.maximum(z, 0.0) + jnp.log(1.0 + jnp.exp(-jnp.abs(z)))


def _mm(a, b):
    return jnp.dot(a, b, preferred_element_type=F32)


def _nt_dot(a, b):
    return lax.dot_general(a, b, (((1,), (1,)), ((), ())), preferred_element_type=F32)


def _rope_apply(x, c_ref, s1_ref, s2_ref, half):
    return (x * c_ref[...]
            + pltpu.roll(x, HEAD_DIM - half, axis=1) * s1_ref[...]
            + pltpu.roll(x, half, axis=1) * s2_ref[...])


def _in_proj_kernel(x_ref, g_ref, w_ref, ca_ref, s1a_ref, s2a_ref, cb_ref, s1b_ref, s2b_ref,
                    o_ref, h_scr):
    j = pl.program_id(1)

    @pl.when(j == 0)
    def _():
        xf = x_ref[...]
        ms = jnp.mean(xf * xf, axis=-1, keepdims=True)
        h_scr[...] = (xf * lax.rsqrt(ms + RMS_EPS) * g_ref[...]).astype(BF16)

    acc = _mm(h_scr[...], w_ref[...])
    tiles_per_section = D_MODEL // IN_TN
    heads_per_tile = IN_TN // HEAD_DIM
    section = j // tiles_per_section
    head0 = (j % tiles_per_section) * heads_per_tile
    rotated = section < 2
    for c in range(heads_per_tile):
        head = head0 + c
        sl = slice(c * HEAD_DIM, (c + 1) * HEAD_DIM)
        xa = acc[:, sl]
        is_a = jnp.logical_and(rotated, head < HEADS_A)
        is_b = jnp.logical_and(rotated, jnp.logical_and(head >= HEADS_A, head < B_END))

        @pl.when(is_a)
        def _():
            o_ref[:, sl] = _rope_apply(xa, ca_ref, s1a_ref, s2a_ref, ROT_A // 2)

        @pl.when(is_b)
        def _():
            o_ref[:, sl] = _rope_apply(xa, cb_ref, s1b_ref, s2b_ref, ROT_B // 2)

        @pl.when(jnp.logical_not(jnp.logical_or(is_a, is_b)))
        def _():
            o_ref[:, sl] = xa


def _in_proj(x2d, g_pre, w_bf16, tables):
    m = x2d.shape[0]
    tm = min(m, IN_TM_MAX)
    n_out = w_bf16.shape[1]
    tab_rows = tables[0].shape[0]
    tab_blocks = tab_rows // tm
    tab_spec = pl.BlockSpec((tm, HEAD_DIM), lambda i, j: (i % tab_blocks, 0))
    return pl.pallas_call(
        _in_proj_kernel,
        grid=(m // tm, n_out // IN_TN),
        in_specs=[
            pl.BlockSpec((tm, D_MODEL), lambda i, j: (i, 0)),
            pl.BlockSpec((1, D_MODEL), lambda i, j: (0, 0)),
            pl.BlockSpec((D_MODEL, IN_TN), lambda i, j: (0, j)),
        ] + [tab_spec] * 6,
        out_specs=pl.BlockSpec((tm, IN_TN), lambda i, j: (i, j)),
        out_shape=jax.ShapeDtypeStruct((m, n_out), F32),
        scratch_shapes=[pltpu.VMEM((tm, D_MODEL), BF16)],
        compiler_params=pltpu.CompilerParams(
            dimension_semantics=("arbitrary", "arbitrary"),
            vmem_limit_bytes=VMEM_LIMIT_BYTES),
        name="in_proj",
    )(x2d, g_pre.reshape(1, D_MODEL), w_bf16, *tables)


def _rope_tables(pos):
    posf = pos.astype(F32)[:, None]
    lane = jnp.arange(HEAD_DIM)

    def build(rot, period):
        half = rot // 2
        inv = ROPE_THETA ** (-jnp.arange(half, dtype=F32) * (2.0 / rot))
        ang = posf * inv[None, :]
        cos, sin = jnp.cos(ang), jnp.sin(ang)
        within = lane % period
        idx = within % half
        first = within < half
        second = jnp.logical_and(within >= half, within < rot)
        cos_l, sin_l = cos[:, idx], sin[:, idx]
        c = jnp.where(jnp.logical_or(first, second)[None, :], cos_l, 1.0)
        s1 = jnp.where(first[None, :], -sin_l, 0.0)
        s2 = jnp.where(second[None, :], sin_l, 0.0)
        return c.astype(F32), s1.astype(F32), s2.astype(F32)

    return build(ROT_A, HEAD_DIM) + build(ROT_B, DIFF_HALF)


def _out_proj_kernel(o_ref, w_ref, g_ref, x_ref, y_ref):
    y = _mm(o_ref[...].astype(BF16), w_ref[...])
    ms = jnp.mean(y * y, axis=-1, keepdims=True)
    y_ref[...] = x_ref[...] + y * lax.rsqrt(ms + RMS_EPS) * g_ref[...]


def _out_proj(o2d, w_bf16, g_post, x2d):
    m = x2d.shape[0]
    tm = min(m, OUT_TM_MAX)
    return pl.pallas_call(
        _out_proj_kernel,
        grid=(m // tm,),
        in_specs=[
            pl.BlockSpec((tm, D_MODEL), lambda i: (i, 0)),
            pl.BlockSpec((D_MODEL, D_MODEL), lambda i: (0, 0)),
            pl.BlockSpec((1, D_MODEL), lambda i: (0, 0)),
            pl.BlockSpec((tm, D_MODEL), lambda i: (i, 0)),
        ],
        out_specs=pl.BlockSpec((tm, D_MODEL), lambda i: (i, 0)),
        out_shape=jax.ShapeDtypeStruct((m, D_MODEL), F32),
        compiler_params=pltpu.CompilerParams(
            dimension_semantics=("arbitrary",),
            vmem_limit_bytes=VMEM_LIMIT_BYTES),
        name="out_proj",
    )(o2d, w_bf16, g_post.reshape(1, D_MODEL), x2d)


def _softmax_probs(s, bias_row, mask, m_ref, b, first):
    if bias_row is not None:
        s = s + bias_row
    if mask is not None:
        s = jnp.where(mask, s, NEG_INF)
    m_blk = jnp.max(s, axis=0, keepdims=True)
    if first:
        m_new, alpha = m_blk, None
    else:
        m_old = m_ref[b]
        m_new = jnp.maximum(m_old, m_blk)
        alpha = jnp.exp(m_old - m_new)
    p = jnp.exp(s - m_new)
    return p.astype(BF16), (m_new, alpha, jnp.sum(p, axis=0, keepdims=True))


def _softmax_update(pv, stats, m_ref, l_ref, acc_ref, b, first):
    m_new, alpha, l_add = stats
    m_ref[b] = m_new
    if first:
        l_ref[b] = l_add
        acc_ref[b] = pv
    else:
        l_ref[b] = alpha * l_ref[b] + l_add
        acc_ref[b] = alpha * acc_ref[b] + pv


def _softmax_group(bs, k_blks, vt_blks, maps, scale, bias_rows, mask, first):
    scores = [[_mm(k_blks[b], qt_scr[b]) * scale for b in bs] for qt_scr, _, _, _ in maps]
    probs = [[_softmax_probs(scores[i][n], None if bias_rows is None else bias_rows[b], mask,
                             maps[i][1], b, first)
              for n, b in enumerate(bs)] for i in range(len(maps))]
    pvs = [[_mm(vt_blks[b], probs[i][n][0]) for n, b in enumerate(bs)] for i in range(len(maps))]
    for i, (_, m_ref, l_ref, acc_ref) in enumerate(maps):
        for n, b in enumerate(bs):
            _softmax_update(pvs[i][n], probs[i][n][1], m_ref, l_ref, acc_ref, b, first)


def _prompt_attn_kernel(lam_ref, q_ref, k_ref, v_ref, gate_ref, gain_ref, o_ref,
                        k_scr, vt_scr, kmean_scr, tri_scr, qt1_scr, qt2_scr, bias_scr,
                        m1, l1, acc1, m2, l2, acc2, *, lam_init):
    head = pl.program_id(0)
    qi = pl.program_id(1)
    nb = q_ref.shape[0]
    n_blk = k_scr.shape[1] // ATT_T
    kidx = lax.broadcasted_iota(jnp.int32, (ATT_T, ATT_T), 0)
    qidx = lax.broadcasted_iota(jnp.int32, (ATT_T, ATT_T), 1)

    @pl.when(qi == 0)
    def _():
        tri_scr[...] = (qidx >= kidx).astype(BF16)
        kmean_scr[...] = jnp.zeros_like(kmean_scr)
        for b in range(nb):
            for j in range(n_blk):
                rows = slice(j * ATT_T, (j + 1) * ATT_T)
                kf = k_ref[b, rows, :]
                k_scr[b, rows, :] = kf.astype(BF16)
                vt_scr[b, j] = v_ref[b, rows, :].T.astype(BF16)
                kmean_scr[b, j:j + 1, :] = jnp.mean(kf, axis=0, keepdims=True)

    diag = pl.ds(pl.multiple_of(qi * ATT_T, ATT_T), ATT_T)

    def blk_rows(j):
        return pl.ds(pl.multiple_of(j * ATT_T, ATT_T), ATT_T)

    def gate_mul(b):
        g = gate_ref[b]
        return g * jax.nn.sigmoid(g)

    def groups(size):
        return [list(range(b0, min(b0 + size, nb))) for b0 in range(0, nb, size)]

    @pl.when(head < HEADS_A)
    def _():
        scale = HEAD_DIM ** -0.5
        for b in range(nb):
            qt = q_ref[b].T
            qt1_scr[b] = qt.astype(BF16)
            kh, kl = _split2(kmean_scr[b])
            qh, ql = _split2(qt)
            g_t = (_mm(kh, qh) + _mm(kh, ql) + _mm(kl, qh))[0:n_blk]
            blk = lax.broadcasted_iota(jnp.int32, (n_blk, ATT_T), 0)
            valid = blk < qi
            g_t = jnp.where(valid, g_t, NEG_INF)
            cnt = jnp.zeros((n_blk, ATT_T), F32)
            for i in range(n_blk):
                gi = g_t[i:i + 1, :]
                beats = jnp.logical_or(gi > g_t, jnp.logical_and(gi == g_t, i < blk))
                cnt = cnt + jnp.where(beats, 1.0, 0.0)
            sel = jnp.logical_and(valid, cnt < float(MOBA_TOPK))
            bias_scr[b] = jnp.where(sel, 0.0, NEG_INF)
        maps = [(qt1_scr, m1, l1, acc1)]
        for bs in groups(MOBA_GROUP):
            _softmax_group(bs, {b: k_scr[b, diag, :] for b in bs}, {b: vt_scr[b, qi] for b in bs},
                           maps, scale, None, kidx <= qidx, True)

        def body(j, carry):
            for bs in groups(MOBA_GROUP):
                _softmax_group(bs, {b: k_scr[b, blk_rows(j), :] for b in bs},
                               {b: vt_scr[b, j] for b in bs}, maps, scale,
                               {b: bias_scr[b, pl.ds(j, 1), :] for b in bs}, None, False)
            return carry

        lax.fori_loop(0, qi, body, 0)
        for b in range(nb):
            o_t = acc1[b] / l1[b]
            o_ref[b] = (o_t.T * gate_mul(b)).astype(o_ref.dtype)

    @pl.when(jnp.logical_and(head >= HEADS_A, head < B_END))
    def _():
        scale = DIFF_HALF ** -0.5
        dim = lax.broadcasted_iota(jnp.int32, (HEAD_DIM, ATT_T), 0)
        for b in range(nb):
            qt = q_ref[b].T
            qt1_scr[b] = jnp.where(dim < DIFF_HALF, qt, 0.0).astype(BF16)
            qt2_scr[b] = jnp.where(dim >= DIFF_HALF, qt, 0.0).astype(BF16)
        maps = [(qt1_scr, m1, l1, acc1), (qt2_scr, m2, l2, acc2)]
        for bs in groups(DIFF_GROUP):
            _softmax_group(bs, {b: k_scr[b, diag, :] for b in bs}, {b: vt_scr[b, qi] for b in bs},
                           maps, scale, None, kidx <= qidx, True)

        def body(j, carry):
            for bs in groups(DIFF_GROUP):
                _softmax_group(bs, {b: k_scr[b, blk_rows(j), :] for b in bs},
                               {b: vt_scr[b, j] for b in bs}, maps, scale, None, None, False)
            return carry

        lax.fori_loop(0, qi, body, 0)
        for b in range(nb):
            d = acc1[b] / l1[b] - lam_ref[0, 0] * (acc2[b] / l2[b])
            ms = jnp.mean(d * d, axis=0, keepdims=True)
            d = (d * lax.rsqrt(ms + RMS_EPS)).T * gain_ref[...] * (1.0 - lam_init)
            o_ref[b] = (d * gate_mul(b)).astype(o_ref.dtype)

    @pl.when(head >= B_END)
    def _():
        scale = HEAD_DIM ** -0.5
        for b in range(nb):
            qt1_scr[b] = q_ref[b].T.astype(BF16)

        def sb_group(bs, k_blks, vt_blks, earlier, first):
            zs = [_mm(k_blks[b], qt1_scr[b]) * scale for b in bs]
            splits = []
            for z in zs:
                log_keep = -_softplus(z)
                if earlier is not None:
                    log_keep = jnp.where(earlier, log_keep, 0.0)
                splits.append(_split2(log_keep))
            incls = [_mm(tri_scr[...], hi) + _mm(tri_scr[...], lo) for hi, lo in splits]
            probs = []
            for n, b in enumerate(bs):
                expo = zs[n] + incls[n]
                if not first:
                    expo = expo + m1[b]
                a = jnp.exp(expo)
                if earlier is not None:
                    a = jnp.where(earlier, a, 0.0)
                probs.append(a.astype(BF16))
            pvs = [_mm(vt_blks[b], probs[n]) for n, b in enumerate(bs)]
            for n, b in enumerate(bs):
                if first:
                    acc1[b] = pvs[n]
                    m1[b] = incls[n][0:1, :]
                else:
                    acc1[b] = acc1[b] + pvs[n]
                    m1[b] = m1[b] + incls[n][0:1, :]

        for bs in groups(SB_GROUP):
            sb_group(bs, {b: k_scr[b, diag, :] for b in bs}, {b: vt_scr[b, qi] for b in bs},
                     kidx < qidx, True)

        def body(t, carry):
            j = qi - 1 - t
            for bs in groups(SB_GROUP):
                sb_group(bs, {b: k_scr[b, blk_rows(j), :] for b in bs},
                         {b: vt_scr[b, j] for b in bs}, None, False)
            return carry

        lax.fori_loop(0, qi, body, 0)
        for b in range(nb):
            o_ref[b] = (acc1[b].T * gate_mul(b)).astype(o_ref.dtype)


def _prompt_attention(proj, lam, diff_gain, bsz, t_len, lam_init):
    nq = t_len // ATT_T
    proj3 = proj.reshape(bsz, t_len, 4 * D_MODEL)
    kernel = functools.partial(_prompt_attn_kernel, lam_init=lam_init)
    stat = pltpu.VMEM((bsz, 1, ATT_T), F32)
    acc = pltpu.VMEM((bsz, HEAD_DIM, ATT_T), F32)
    out = pl.pallas_call(
        kernel,
        grid=(N_HEADS, nq),
        in_specs=[
            pl.BlockSpec(memory_space=pltpu.SMEM),
            pl.BlockSpec((bsz, ATT_T, HEAD_DIM), lambda h, i: (0, i, h)),
            pl.BlockSpec((bsz, t_len, HEAD_DIM), lambda h, i: (0, 0, N_HEADS + h)),
            pl.BlockSpec((bsz, t_len, HEAD_DIM), lambda h, i: (0, 0, 2 * N_HEADS + h)),
            pl.BlockSpec((bsz, ATT_T, HEAD_DIM), lambda h, i: (0, i, 3 * N_HEADS + h)),
            pl.BlockSpec((1, HEAD_DIM), lambda h, i: (0, 0)),
        ],
        out_specs=pl.BlockSpec((bsz, ATT_T, HEAD_DIM), lambda h, i: (0, i, h)),
        out_shape=jax.ShapeDtypeStruct((bsz, t_len, D_MODEL), BF16),
        scratch_shapes=[
            pltpu.VMEM((bsz, t_len, HEAD_DIM), BF16),
            pltpu.VMEM((bsz, nq, HEAD_DIM, ATT_T), BF16),
            pltpu.VMEM((bsz, 2 * SUBLANES, HEAD_DIM), F32),
            pltpu.VMEM((ATT_T, ATT_T), BF16),
            pltpu.VMEM((bsz, HEAD_DIM, ATT_T), BF16),
            pltpu.VMEM((bsz, HEAD_DIM, ATT_T), BF16),
            pltpu.VMEM((bsz, nq, ATT_T), F32),
            stat, stat, acc, stat, stat, acc,
        ],
        compiler_params=pltpu.CompilerParams(
            dimension_semantics=("arbitrary", "arbitrary"),
            vmem_limit_bytes=VMEM_LIMIT_BYTES),
        name="prompt_attn",
    )(lam.reshape(1, 1), proj3, proj3, proj3, proj3, diff_gain.reshape(1, HEAD_DIM))
    return out.reshape(bsz * t_len, D_MODEL)


def _decode_attn_kernel(pt_ref, lam_ref, q_ref, kn_ref, vn_ref, gate_ref, gain_ref, *rest,
                        lam_init, n_steps, page):
    del pt_ref
    pps = DEC_PAGES_PER_STEP
    k_refs = rest[0:pps]
    v_refs = rest[pps:2 * pps]
    (o_ref, qs_scr, qblk_scr, bias_scr, m_scr, l_scr, acc_scr, ksum_scr,
     mo_m, mo_l, mo_o) = rest[2 * pps:]
    step = pl.program_id(1)
    n_pages = n_steps * pps
    lanes = page * N_HEADS
    n_tiles = lanes // HEAD_DIM
    pages_per_blk = MOBA_BLOCK // page
    n_blk = n_pages // pages_per_blk
    T0, T1, T2, T3 = (slice(i * SUBLANES, (i + 1) * SUBLANES) for i in range(4))

    rows = lax.broadcasted_iota(jnp.int32, (DEC_ROWS, 1), 0)
    tile_of_row = rows >> 3
    row_head = jnp.where(tile_of_row == 0, rows,
                         jnp.where(tile_of_row == 3, B_END + (rows & 7), HEADS_A + (rows & 7)))

    @pl.when(step == 0)
    def _():
        is_diff = jnp.logical_or(tile_of_row == 1, tile_of_row == 2)
        qs = q_ref[...] * jnp.where(is_diff, DIFF_HALF ** -0.5, HEAD_DIM ** -0.5)
        qs_scr[...] = qs
        zero = jnp.zeros_like(qs)
        qblk_scr[...] = jnp.concatenate(
            [jnp.concatenate([qs, zero], axis=1), jnp.concatenate([zero, qs], axis=1)],
            axis=0).astype(BF16)
        c2 = lax.broadcasted_iota(jnp.int32, (DEC_ROWS, lanes), 1)
        bias_scr[...] = jnp.where((c2 & (N_HEADS - 1)) == row_head, 0.0, NEG_INF)
        m_scr[...] = jnp.full_like(m_scr, NEG_INF)
        l_scr[...] = jnp.zeros_like(l_scr)
        acc_scr[...] = jnp.zeros_like(acc_scr)

    qblk = qblk_scr[...]
    s_pages = []
    for t in range(0, pps, 2):
        k2 = jnp.concatenate([k_refs[t][...].astype(BF16), k_refs[t + 1][...].astype(BF16)], axis=1)
        s2 = _nt_dot(qblk, k2)
        s_pages += [s2[0:DEC_ROWS], s2[DEC_ROWS:2 * DEC_ROWS]]
    for t in range(pps):
        k3 = k_refs[t][...].reshape(page, N_HEADS, HEAD_DIM)
        ksum_scr[step * pps + t] = jnp.sum(k3[:, 0:SUBLANES, :], axis=0)

    bias = bias_scr[...]
    p0, m0, l0 = [], [], []
    for t in range(pps):
        s0 = s_pages[t][T0] + bias[T0]
        m_pg = jnp.max(s0, axis=1, keepdims=True)
        p_pg = jnp.exp(s0 - m_pg)
        p0.append(p_pg)
        m0.append(m_pg)
        l0.append(jnp.sum(p_pg, axis=1, keepdims=True))
    dsl = slice(SUBLANES, 3 * SUBLANES)
    sd = [s_pages[t][dsl] + bias[dsl] for t in range(pps)]
    m_old = m_scr[...]
    m_new = m_old
    for t in range(pps):
        m_new = jnp.maximum(m_new, jnp.max(sd[t], axis=1, keepdims=True))
    alpha_d = jnp.exp(m_old - m_new)
    pd = [jnp.exp(sd[t] - m_new) for t in range(pps)]
    l_new = alpha_d * l_scr[...]
    for t in range(pps):
        l_new = l_new + jnp.sum(pd[t], axis=1, keepdims=True)
    m_scr[...] = m_new
    l_scr[...] = l_new

    lane_t = lax.broadcasted_iota(jnp.int32, (SUBLANES, HEAD_DIM), 1)
    row_t = lax.broadcasted_iota(jnp.int32, (SUBLANES, HEAD_DIM), 0)
    own3 = (lane_t & (N_HEADS - 1)) == B_END + row_t
    z_tiles, lk_tiles = [], []
    for t in range(pps):
        z = s_pages[t][T3]
        sp = _softplus(z)
        for i in range(n_tiles):
            ls = slice(i * HEAD_DIM, (i + 1) * HEAD_DIM)
            z_tiles.append(z[:, ls])
            lk_tiles.append(jnp.where(own3, -sp[:, ls], 0.0))
    lk_all = jnp.concatenate(lk_tiles, axis=0)
    hi, lo = _split2(lk_all)
    kr = lax.broadcasted_iota(jnp.int32, (HEAD_DIM, HEAD_DIM), 0)
    kc = lax.broadcasted_iota(jnp.int32, (HEAD_DIM, HEAD_DIM), 1)
    tri = (kr >= kc).astype(BF16)
    incl_all = _mm(hi, tri) + _mm(lo, tri)
    later = jnp.zeros((SUBLANES, 1), F32)
    a_tiles = [None] * (pps * n_tiles)
    for idx in reversed(range(pps * n_tiles)):
        incl = incl_all[idx * SUBLANES:(idx + 1) * SUBLANES]
        a_tiles[idx] = jnp.where(own3, jnp.exp(z_tiles[idx] + incl + later), 0.0)
        later = later + incl[:, 0:1]
    alpha = jnp.concatenate([alpha_d, jnp.exp(later)], axis=0)

    pv_rest = jnp.zeros((3 * SUBLANES, HEAD_DIM), F32)
    for t in range(pps):
        a_pg = jnp.concatenate(a_tiles[t * n_tiles:(t + 1) * n_tiles], axis=1)
        probs = jnp.concatenate([p0[t], pd[t], a_pg], axis=0).astype(BF16)
        pv = _mm(probs, v_refs[t][...].astype(BF16))
        pg = step * pps + t
        mo_m[pg] = jnp.broadcast_to(m0[t], (SUBLANES, HEAD_DIM))
        mo_l[pg] = jnp.broadcast_to(l0[t], (SUBLANES, HEAD_DIM))
        mo_o[pg] = pv[T0]
        pv_rest = pv_rest + pv[SUBLANES:DEC_ROWS]
    acc_scr[...] = alpha * acc_scr[...] + pv_rest

    @pl.when(step == n_steps - 1)
    def _():
        qs = qs_scr[...]
        vn = vn_ref[...]
        s_new = jnp.sum(qs * kn_ref[...], axis=1, keepdims=True)

        m_prev = m_scr[...]
        s_d = s_new[dsl]
        m_fin = jnp.maximum(m_prev, s_d)
        a_fin = jnp.exp(m_prev - m_fin)
        p_new = jnp.exp(s_d - m_fin)
        l_fin = a_fin * l_scr[...] + p_new
        acc = acc_scr[...]
        dn = (a_fin * acc[0:2 * SUBLANES] + p_new * vn[dsl]) / l_fin
        d = dn[0:SUBLANES] - lam_ref[0, 0] * dn[SUBLANES:2 * SUBLANES]
        ms = jnp.mean(d * d, axis=1, keepdims=True)
        d = d * lax.rsqrt(ms + RMS_EPS) * gain_ref[...] * (1.0 - lam_init)

        q8 = q_ref[T0, :]
        gates = []
        for blk in range(n_blk):
            ksum = ksum_scr[blk * pages_per_blk]
            for u in range(1, pages_per_blk):
                ksum = ksum + ksum_scr[blk * pages_per_blk + u]
            gates.append(jnp.sum(q8 * (ksum * (1.0 / MOBA_BLOCK)), axis=1, keepdims=True))
        sels = []
        for blk in range(n_blk):
            cnt = jnp.zeros((SUBLANES, 1), F32)
            for i in range(n_blk):
                if i == blk:
                    continue
                beats = gates[i] > gates[blk]
                if i < blk:
                    beats = jnp.logical_or(beats, gates[i] == gates[blk])
                cnt = cnt + jnp.where(beats, 1.0, 0.0)
            sels.append(cnt < float(MOBA_TOPK))
        s8 = s_new[T0]
        m_tot = s8
        for pg in range(n_pages):
            m_tot = jnp.maximum(
                m_tot, jnp.where(sels[pg // pages_per_blk], mo_m[pg][:, 0:1], NEG_INF))
        w_new = jnp.exp(s8 - m_tot)
        num = w_new * vn[T0]
        den = w_new
        for pg in range(n_pages):
            w = jnp.where(sels[pg // pages_per_blk], jnp.exp(mo_m[pg][:, 0:1] - m_tot), 0.0)
            num = num + w * mo_o[pg]
            den = den + w * mo_l[pg][:, 0:1]
        om = num / den

        o_rows = jnp.concatenate(
            [om, d, jnp.zeros((SUBLANES, HEAD_DIM), F32), acc[2 * SUBLANES:3 * SUBLANES]], axis=0)
        gate = gate_ref[...]
        o_ref[...] = o_rows * (gate * jax.nn.sigmoid(gate))


def _decode_attention(proj, cache_k, cache_v, page_table, layer, lam, diff_gain, lam_init):
    n_rows = proj.shape[0]
    n_pages = page_table.shape[1]
    depth, n_pool, page = cache_k.shape[0:3]
    pps = DEC_PAGES_PER_STEP
    n_steps = n_pages // pps
    lanes = page * N_HEADS
    ck = cache_k.reshape(depth, n_pool, lanes, HEAD_DIM)
    cv = cache_v.reshape(depth, n_pool, lanes, HEAD_DIM)

    sections = proj.reshape(n_rows, 4, N_HEADS, HEAD_DIM)[:, :, DEC_ROW_HEAD, :]
    dim = np.arange(HEAD_DIM)
    tile = np.arange(DEC_ROWS) // SUBLANES
    qmask = DEC_ROW_VALID[:, None] & np.where(
        tile[:, None] == 1, dim[None, :] < DIFF_HALF,
        np.where(tile[:, None] == 2, dim[None, :] >= DIFF_HALF, True))
    q32 = jnp.where(jnp.asarray(qmask)[None], sections[:, 0], 0.0)
    kn32, vn32, gate32 = sections[:, 1], sections[:, 2], sections[:, 3]

    kernel = functools.partial(_decode_attn_kernel, lam_init=lam_init, n_steps=n_steps, page=page)

    def row_spec():
        return pl.BlockSpec((None, DEC_ROWS, HEAD_DIM), lambda b, s, pt: (b, 0, 0))

    def page_spec(t):
        return pl.BlockSpec((None, None, lanes, HEAD_DIM),
                            lambda b, s, pt: (layer, pt[b * n_pages + s * pps + t], 0, 0))

    grid_spec = pltpu.PrefetchScalarGridSpec(
        num_scalar_prefetch=1,
        grid=(n_rows, n_steps),
        in_specs=[pl.BlockSpec(memory_space=pltpu.SMEM),
                  row_spec(), row_spec(), row_spec(), row_spec(),
                  pl.BlockSpec((1, HEAD_DIM), lambda b, s, pt: (0, 0))]
        + [page_spec(t) for t in range(pps)] * 2,
        out_specs=row_spec(),
        scratch_shapes=[
            pltpu.VMEM((DEC_ROWS, HEAD_DIM), F32),
            pltpu.VMEM((2 * DEC_ROWS, 2 * HEAD_DIM), BF16),
            pltpu.VMEM((DEC_ROWS, lanes), F32),
            pltpu.VMEM((2 * SUBLANES, 1), F32),
            pltpu.VMEM((2 * SUBLANES, 1), F32),
            pltpu.VMEM((3 * SUBLANES, HEAD_DIM), F32),
            pltpu.VMEM((n_pages, SUBLANES, HEAD_DIM), F32),
            pltpu.VMEM((n_pages, SUBLANES, HEAD_DIM), F32),
            pltpu.VMEM((n_pages, SUBLANES, HEAD_DIM), F32),
            pltpu.VMEM((n_pages, SUBLANES, HEAD_DIM), F32),
        ],
    )
    out = pl.pallas_call(
        kernel,
        grid_spec=grid_spec,
        out_shape=jax.ShapeDtypeStruct((n_rows, DEC_ROWS, HEAD_DIM), F32),
        compiler_params=pltpu.CompilerParams(
            dimension_semantics=("arbitrary", "arbitrary"),
            vmem_limit_bytes=VMEM_LIMIT_BYTES),
        name="decode_attn",
    )(page_table.reshape(-1), lam.reshape(1, 1), q32, kn32, vn32, gate32,
      diff_gain.reshape(1, HEAD_DIM), *([ck] * pps), *([cv] * pps))
    heads = jnp.concatenate(
        [out[:, 0:HEADS_A], out[:, SUBLANES:SUBLANES + HEADS_B],
         out[:, 3 * SUBLANES:3 * SUBLANES + HEADS_C]], axis=1)
    return heads.reshape(n_rows, D_MODEL)


def kernel(x_prompt, x_sample, cache_k, cache_v, page_table, g_pre, w_in, w_out, g_post,
           lam_q1, lam_k1, lam_q2, lam_k2, diff_gain):
    bsz, t_len, _ = x_prompt.shape
    dec_b, dec_t, _ = x_sample.shape
    depth, _, page, _, _ = cache_k.shape
    n_pages = page_table.shape[1]
    past_len = n_pages * page
    assert dec_t == 1 and t_len % ATT_T == 0 and MOBA_BLOCK % page == 0
    assert past_len % MOBA_BLOCK == 0 and n_pages % DEC_PAGES_PER_STEP == 0
    assert (page * N_HEADS) % (2 * HEAD_DIM) == 0

    tab_p = _rope_tables(jnp.arange(t_len, dtype=jnp.int32))
    tab_s = _rope_tables(jnp.full((dec_b,), past_len, dtype=jnp.int32))

    y_p = x_prompt.reshape(bsz * t_len, D_MODEL)
    y_s = x_sample.reshape(dec_b, D_MODEL)
    k_p, v_p, k_s, v_s = [], [], [], []
    for layer in range(depth):
        lam_init = 0.8 - 0.6 * math.exp(-0.3 * layer)
        lam = (jnp.exp(jnp.sum(lam_q1[layer].astype(F32) * lam_k1[layer].astype(F32)))
               - jnp.exp(jnp.sum(lam_q2[layer].astype(F32) * lam_k2[layer].astype(F32)))
               + lam_init)
        w_in_b = w_in[layer].astype(BF16)
        w_out_b = w_out[layer].astype(BF16)

        proj_p = _in_proj(y_p, g_pre[layer], w_in_b, tab_p)
        o_p = _prompt_attention(proj_p, lam, diff_gain[layer], bsz, t_len, lam_init)
        y_p = _out_proj(o_p, w_out_b, g_post[layer], y_p)
        k_p.append(proj_p[:, D_MODEL:2 * D_MODEL].reshape(bsz, t_len, N_HEADS, HEAD_DIM))
        v_p.append(proj_p[:, 2 * D_MODEL:3 * D_MODEL].reshape(bsz, t_len, N_HEADS, HEAD_DIM))

        proj_s = _in_proj(y_s, g_pre[layer], w_in_b, tab_s)
        o_s = _decode_attention(proj_s, cache_k, cache_v, page_table, layer, lam,
                                diff_gain[layer], lam_init)
        y_s = _out_proj(o_s, w_out_b, g_post[layer], y_s)
        k_s.append(proj_s[:, D_MODEL:2 * D_MODEL].reshape(dec_b, dec_t, N_HEADS, HEAD_DIM))
        v_s.append(proj_s[:, 2 * D_MODEL:3 * D_MODEL].reshape(dec_b, dec_t, N_HEADS, HEAD_DIM))

    return (y_p.reshape(bsz, t_len, D_MODEL), y_s.reshape(dec_b, dec_t, D_MODEL),
            jnp.stack(k_p), jnp.stack(v_p), jnp.stack(k_s), jnp.stack(v_s))
```
